```python
import jax, jax.numpy as jnp
from jax import lax
import numpy as np

D_MODEL = 1024
BATCH = 8
SEQ = 2048
DEPTH = 4

MIX_WIDTH = D_MODEL
SGU_CHUNK = 128
SGU_WIDTH = MIX_WIDTH // 2
SGU_GROUPS = 4
SGU_GROUP_DIM = SGU_WIDTH // SGU_GROUPS
ATT_WIDTH = MIX_WIDTH - SGU_WIDTH
ATT_HEAD_DIM = 64
ATT_HEADS = ATT_WIDTH // ATT_HEAD_DIM
IDX_HEADS = 4
IDX_HEAD_DIM = 64
TOPK_MAX = 256
ROPE_THETA = 500000.0
ROPE_FRACTION = 4
D_FF = 2816
Q_BLOCK = 128
RMS_EPS = 1e-6
N_MOD = 9

OFF_SGU_U = 0
OFF_SGU_V = OFF_SGU_U + SGU_WIDTH
OFF_Q = OFF_SGU_V + SGU_WIDTH
OFF_K = OFF_Q + ATT_WIDTH
OFF_V = OFF_K + ATT_WIDTH
OFF_IQ = OFF_V + ATT_WIDTH
OFF_IK = OFF_IQ + IDX_HEADS * IDX_HEAD_DIM
OFF_IW = OFF_IK + IDX_HEAD_DIM
PROJ_WIDTH = OFF_IW + IDX_HEADS

kernel_name = "hymba_gmlp_dsa_macaron_adaln"


def rms_norm(x, g):
    xf = x.astype(jnp.float32)
    y = xf * lax.rsqrt(jnp.mean(xf * xf, axis=-1, keepdims=True) + RMS_EPS)
    return (y * g.astype(jnp.float32)).astype(x.dtype)


def modulate(h, shift, scale):
    return h * (1 + scale[:, None, :]) + shift[:, None, :]


def swiglu(h, w_in, w_out):
    g, u = jnp.split(h @ w_in, 2, axis=-1)
    return (jax.nn.silu(g) * u) @ w_out


def rope_tables(positions, rot_dim):
    inv_freq = ROPE_THETA ** (-jnp.arange(0, rot_dim, 2, dtype=jnp.float32) / rot_dim)
    ang = positions.astype(jnp.float32)[..., None] * inv_freq
    return jnp.cos(ang)[:, :, None, :], jnp.sin(ang)[:, :, None, :]


def partial_rope(x, cos, sin):
    half = cos.shape[-1]
    rot = 2 * half
    xr = x[..., :rot].astype(jnp.float32)
    x1, x2 = xr[..., :half], xr[..., half:]
    out = jnp.concatenate([x1 * cos - x2 * sin, x2 * cos + x1 * sin], axis=-1).astype(x.dtype)
    return jnp.concatenate([out, x[..., rot:]], axis=-1)


def chunked_sgu(u, v, w_s, b_s):
    B, S, _ = u.shape
    n_chunk = S // SGU_CHUNK
    v = v.reshape(B, n_chunk, SGU_CHUNK, SGU_GROUPS, SGU_GROUP_DIM)
    causal = jnp.tril(jnp.ones((SGU_CHUNK, SGU_CHUNK), dtype=bool))
    w = jnp.where(causal[None], w_s, 0).astype(v.dtype)
    mixed = jnp.einsum('gts,bnsgc->bntgc', w, v) + b_s.T[None, None, :, :, None].astype(v.dtype)
    return u * mixed.reshape(B, S, SGU_WIDTH)


def dsa_attention(q, k, v, qi, ki, wi):
    B, S = q.shape[0], q.shape[1]
    top_k = min(TOPK_MAX, S // 4)
    n_blk = S // Q_BLOCK

    def to_blocks(a):
        return a.reshape((B, n_blk, Q_BLOCK) + a.shape[2:]).swapaxes(0, 1)

    t_blocks = jnp.arange(S, dtype=jnp.int32).reshape(n_blk, Q_BLOCK)
    key_pos = jnp.arange(S, dtype=jnp.int32)
    ki_f = ki.astype(jnp.float32)
    scale = ATT_HEAD_DIM ** -0.5

    def one_block(args):
        qb, qib, wib, tb = args
        logits = jax.nn.relu(jnp.einsum('bqhd,bsd->bqhs', qib.astype(jnp.float32), ki_f))
        score = jnp.einsum('bqh,bqhs->bqs', wib.astype(jnp.float32), logits)
        causal = key_pos[None, :] <= tb[:, None]
        score = jnp.where(causal[None], score, -jnp.inf)
        _, idx = lax.top_k(score, top_k)
        valid = idx <= tb[None, :, None]
        kg = jax.vmap(lambda kb, ib: kb[ib])(k, idx)
        vg = jax.vmap(lambda vb, ib: vb[ib])(v, idx)
        s = jnp.einsum('bqhd,bqkhd->bhqk', qb, kg).astype(jnp.float32) * scale
        s = jnp.where(valid[:, None], s, -jnp.inf)
        p = jax.nn.softmax(s, axis=-1).astype(vg.dtype)
        return jnp.einsum('bhqk,bqkhd->bqhd', p, vg)

    out = lax.map(one_block, (to_blocks(q), to_blocks(qi), to_blocks(wi), t_blocks))
    return out.swapaxes(0, 1).reshape(B, S, ATT_WIDTH)


def token_mix(h, cos, sin, w_in, sgu_w, sgu_b, w_out):
    B, S, _ = h.shape
    p = h @ w_in
    u = jax.nn.gelu(p[..., OFF_SGU_U:OFF_SGU_V], approximate=False)
    va = jax.nn.gelu(p[..., OFF_SGU_V:OFF_Q], approximate=False)
    a_out = chunked_sgu(u, va, sgu_w, sgu_b)
    q = partial_rope(p[..., OFF_Q:OFF_K].reshape(B, S, ATT_HEADS, ATT_HEAD_DIM), cos, sin)
    k = partial_rope(p[..., OFF_K:OFF_V].reshape(B, S, ATT_HEADS, ATT_HEAD_DIM), cos, sin)
    v = p[..., OFF_V:OFF_IQ].reshape(B, S, ATT_HEADS, ATT_HEAD_DIM)
    qi = partial_rope(p[..., OFF_IQ:OFF_IK].reshape(B, S, IDX_HEADS, IDX_HEAD_DIM), cos, sin)
    ki = partial_rope(p[..., OFF_IK:OFF_IW][:, :, None, :], cos, sin)[:, :, 0, :]
    wi = p[..., OFF_IW:PROJ_WIDTH]
    b_out = dsa_attention(q, k, v, qi, ki, wi)
    return jnp.concatenate([a_out, b_out], axis=-1) @ w_out


def setup_inputs(seed: int = 0) -> dict:
    key = jax.random.key(seed)
    ks = jax.random.split(key, 24)
    f32 = jnp.float32
    L, D = DEPTH, D_MODEL

    def nrm(k, shape, scale):
        return jax.random.normal(k, shape, f32) * scale

    x = jax.random.normal(ks[0], (BATCH, SEQ, D), f32)
    c = jax.random.normal(ks[1], (BATCH, D), f32)
    positions = (jnp.arange(SEQ, dtype=jnp.int32)[None, :]
                 + jax.random.randint(ks[2], (BATCH, 1), 0, 4096, dtype=jnp.int32))
    return {
        "x": x,
        "c": c,
        "positions": positions,
        "ada_w": nrm(ks[3], (L, D, N_MOD * D), 0.5 * D ** -0.5),
        "ada_b": nrm(ks[4], (L, N_MOD * D), 0.01),
        "norm_ffn1": 1.0 + nrm(ks[5], (L, D), 0.02),
        "ffn1_w_in": nrm(ks[6], (L, D, 2 * D_FF), D ** -0.5),
        "ffn1_w_out": nrm(ks[7], (L, D_FF, D), D_FF ** -0.5),
        "norm_mix": 1.0 + nrm(ks[8], (L, D), 0.02),
        "mix_w_in": nrm(ks[9], (L, D, PROJ_WIDTH), D ** -0.5),
        "sgu_w": nrm(ks[10], (L, SGU_GROUPS, SGU_CHUNK, SGU_CHUNK), SGU_CHUNK ** -0.5),
        "sgu_b": 1.0 + nrm(ks[11], (L, SGU_GROUPS, SGU_CHUNK), 0.02),
        "mix_w_out": nrm(ks[12], (L, MIX_WIDTH, D), MIX_WIDTH ** -0.5),
        "norm_ffn2": 1.0 + nrm(ks[13], (L, D), 0.02),
        "ffn2_w_in": nrm(ks[14], (L, D, 2 * D_FF), D ** -0.5),
        "ffn2_w_out": nrm(ks[15], (L, D_FF, D), D_FF ** -0.5),
        "final_norm": 1.0 + nrm(ks[16], (D,), 0.02),
    }


def reference(x, c, positions, ada_w, ada_b, norm_ffn1, ffn1_w_in, ffn1_w_out,
              norm_mix, mix_w_in, sgu_w, sgu_b, mix_w_out,
              norm_ffn2, ffn2_w_in, ffn2_w_out, final_norm):
    cos, sin = rope_tables(positions, ATT_HEAD_DIM // ROPE_FRACTION)
    cos, sin = cos.astype(x.dtype), sin.astype(x.dtype)
    c_act = jax.nn.silu(c)
    for l in range(DEPTH):
        mod = c_act @ ada_w[l] + ada_b[l]
        sh1, sc1, g1, sh2, sc2, g2, sh3, sc3, g3 = jnp.split(mod, N_MOD, axis=-1)
        h = modulate(rms_norm(x, norm_ffn1[l]), sh1, sc1)
        x = x + 0.5 * g1[:, None, :] * swiglu(h, ffn1_w_in[l], ffn1_w_out[l])
        h = modulate(rms_norm(x, norm_mix[l]), sh2, sc2)
        x = x + g2[:, None, :] * token_mix(h, cos, sin, mix_w_in[l], sgu_w[l], sgu_b[l], mix_w_out[l])
        h = modulate(rms_norm(x, norm_ffn2[l]), sh3, sc3)
        x = x + 0.5 * g3[:, None, :] * swiglu(h, ffn2_w_in[l], ffn2_w_out[l])
    return rms_norm(x, final_norm)
```

```python
import functools

import jax
import jax.numpy as jnp
from jax import lax
from jax.experimental import pallas as pl
from jax.experimental.pallas import tpu as pltpu

F32 = jnp.float32
BF16 = jnp.bfloat16
I32 = jnp.int32

SGU_CHUNK = 128
SGU_GROUPS = 4
SGU_GROUP_DIM = 128
SGU_WIDTH = SGU_GROUPS * SGU_GROUP_DIM
ATT_HEAD_DIM = 64
ATT_HEADS = 8
ATT_WIDTH = ATT_HEADS * ATT_HEAD_DIM
IDX_HEADS = 4
IDX_HEAD_DIM = 64
TOPK_MAX = 256
ROPE_THETA = 500000.0
ROPE_DIM = ATT_HEAD_DIM // 4
ROPE_HALF = ROPE_DIM // 2
RMS_EPS = 1e-6
N_MOD = 9

OFF_SGU_U = 0
OFF_SGU_V = OFF_SGU_U + SGU_WIDTH
OFF_Q = OFF_SGU_V + SGU_WIDTH
OFF_K = OFF_Q + ATT_WIDTH
OFF_V = OFF_K + ATT_WIDTH
OFF_IQ = OFF_V + ATT_WIDTH
OFF_IK = OFF_IQ + IDX_HEADS * IDX_HEAD_DIM
OFF_IW = OFF_IK + IDX_HEAD_DIM
PROJ_WIDTH = OFF_IW + IDX_HEADS

LANES = 128
FF_CHUNK = 256
TOKEN_BLOCK = 512
ATT_BLOCK = 256
IDX_PAD = 384
VMEM_LIMIT = 52 * 1024 * 1024

INT_MIN = -2147483648
NEG_BIAS = -1e30


def _resident(block_shape, index_map):
    return pl.BlockSpec(block_shape, index_map, pipeline_mode=pl.Buffered(1))


def _rms_norm(x, w):
    ms = jnp.mean(x * x, axis=-1, keepdims=True)
    return x * lax.rsqrt(ms + RMS_EPS) * w


def _gelu(x):
    return 0.5 * x * (1.0 + lax.erf(x * (0.5 ** 0.5)))


def _split_bf16(x):
    hi = x.astype(BF16)
    lo = (x - hi.astype(F32)).astype(BF16)
    return hi, lo


def _ada_kernel(c_ref, w_ref, b_ref, o_ref):
    ca = jax.nn.silu(c_ref[...]).astype(BF16)
    w = w_ref[...].astype(BF16)
    o_ref[...] = jnp.dot(ca, w, preferred_element_type=F32) + b_ref[...]


def _ada_call(c, ada_w, ada_b):
    n_layer, d, n_out = ada_w.shape
    b = c.shape[0]
    tn = n_out // 4
    return pl.pallas_call(
        _ada_kernel,
        grid=(n_layer, n_out // tn),
        in_specs=[
            pl.BlockSpec((b, d), lambda l, n: (0, 0)),
            pl.BlockSpec((None, d, tn), lambda l, n: (l, 0, n)),
            pl.BlockSpec((None, 1, tn), lambda l, n: (l, 0, n)),
        ],
        out_specs=pl.BlockSpec((None, b, tn), lambda l, n: (l, 0, n)),
        out_shape=jax.ShapeDtypeStruct((n_layer, b, n_out), F32),
        compiler_params=pltpu.CompilerParams(
            dimension_semantics=("arbitrary", "arbitrary"),
            vmem_limit_bytes=VMEM_LIMIT),
        name="ada_mod",
    )(c, ada_w, ada_b.reshape(n_layer, 1, n_out))


def _ffn_kernel(*refs, mod_row, has_mix, has_final, n_chunks):
    it = iter(refs)
    x_ref, mod_ref, nw_ref, win_ref, wout_ref = (next(it) for _ in range(5))
    if has_mix:
        a_ref, b_ref, wmix_ref = (next(it) for _ in range(3))
    if has_final:
        fw_ref = next(it)
    o_ref = next(it)
    acc_ref = next(it)

    x = x_ref[...]
    if has_mix:
        half = a_ref.shape[-1]
        y = jnp.dot(a_ref[...], wmix_ref[0:half, :], preferred_element_type=F32)
        y = y + jnp.dot(b_ref[...], wmix_ref[half:, :], preferred_element_type=F32)
        x = x + mod_ref[5:6, :] * y

    shift = mod_ref[mod_row:mod_row + 1, :]
    scale = mod_ref[mod_row + 1:mod_row + 2, :]
    gate = mod_ref[mod_row + 2:mod_row + 3, :]
    h = _rms_norm(x, nw_ref[...]) * (1.0 + scale) + shift
    hb = h.astype(BF16)

    acc_ref[...] = jnp.zeros_like(acc_ref)

    def chunk(ci, carry):
        gu = jnp.dot(hb, win_ref[ci], preferred_element_type=F32)
        g = gu[:, :FF_CHUNK]
        u = gu[:, FF_CHUNK:]
        a = (g * jax.nn.sigmoid(g) * u).astype(BF16)
        acc_ref[...] += jnp.dot(a, wout_ref[ci], preferred_element_type=F32)
        return carry

    lax.fori_loop(0, n_chunks, chunk, 0)

    out = x + 0.5 * gate * acc_ref[...]
    if has_final:
        out = _rms_norm(out, fw_ref[...])
    o_ref[...] = out


def _ffn_call(x, mod, layer, mod_row, norm_w, w_in_r, w_out_r, mix=None, final_w=None):
    b, s, d = x.shape
    tm = min(TOKEN_BLOCK, s)
    n_chunks = w_in_r.shape[1]
    has_mix = mix is not None
    has_final = final_w is not None

    tok = lambda bi, i: (bi, i, 0)
    in_specs = [
        pl.BlockSpec((None, tm, d), tok),
        pl.BlockSpec((None, None, N_MOD, d), lambda bi, i: (layer, bi, 0, 0)),
        pl.BlockSpec((None, 1, d), lambda bi, i: (layer, 0, 0)),
        _resident((None, n_chunks, d, 2 * FF_CHUNK), lambda bi, i: (layer, 0, 0, 0)),
        _resident((None, n_chunks, FF_CHUNK, d), lambda bi, i: (layer, 0, 0, 0)),
    ]
    args = [x, mod, norm_w, w_in_r, w_out_r]
    if has_mix:
        a_out, b_out, w_mix = mix
        in_specs += [
            pl.BlockSpec((None, tm, a_out.shape[-1]), tok),
            pl.BlockSpec((None, tm, b_out.shape[-1]), tok),
            _resident((None, w_mix.shape[1], d), lambda bi, i: (layer, 0, 0)),
        ]
        args += [a_out, b_out, w_mix]
    if has_final:
        in_specs.append(pl.BlockSpec((1, d), lambda bi, i: (0, 0)))
        args.append(final_w)

    return pl.pallas_call(
        functools.partial(_ffn_kernel, mod_row=mod_row, has_mix=has_mix,
                          has_final=has_final, n_chunks=n_chunks),
        grid=(b, s // tm),
        in_specs=in_specs,
        out_specs=pl.BlockSpec((None, tm, d), tok),
        out_shape=jax.ShapeDtypeStruct((b, s, d), F32),
        scratch_shapes=[pltpu.VMEM((tm, d), F32)],
        compiler_params=pltpu.CompilerParams(
            dimension_semantics=("parallel", "parallel"),
            vmem_limit_bytes=VMEM_LIMIT),
        name="ffn_mix" if has_mix else "ffn",
    )(*args)


def _rope(p, tab_c, tab_s1, tab_s2):
    width = p.shape[-1]
    reps = width // LANES
    if reps > 1:
        tab_c = jnp.concatenate([tab_c] * reps, axis=1)
        tab_s1 = jnp.concatenate([tab_s1] * reps, axis=1)
        tab_s2 = jnp.concatenate([tab_s2] * reps, axis=1)
    up = pltpu.roll(p, width - ROPE_HALF, 1)
    down = pltpu.roll(p, ROPE_HALF, 1)
    return p * tab_c + up * tab_s1 + down * tab_s2


def _mix_kernel(x_ref, mod_ref, nw_ref, wm_ref, wi_ref, sw_ref, sb_ref, rope_ref,
                a_ref, q_ref, k_ref, v_ref, qi_ref, ki_ref, wiT_ref):
    tm = x_ref.shape[0]
    x = x_ref[...]
    h = _rms_norm(x, nw_ref[...]) * (1.0 + mod_ref[4:5, :]) + mod_ref[3:4, :]
    hb, hlo = _split_bf16(h)

    tab_c = rope_ref[0]
    tab_s1 = rope_ref[1]
    tab_s2 = rope_ref[2]

    u = _gelu(jnp.dot(hb, wm_ref[:, OFF_SGU_U:OFF_SGU_V], preferred_element_type=F32))
    va = _gelu(jnp.dot(hb, wm_ref[:, OFF_SGU_V:OFF_Q], preferred_element_type=F32)).astype(BF16)
    row = lax.broadcasted_iota(I32, (SGU_CHUNK, SGU_CHUNK), 0)
    col = lax.broadcasted_iota(I32, (SGU_CHUNK, SGU_CHUNK), 1)
    causal = col <= row
    for g in range(SGU_GROUPS):
        wg = jnp.where(causal, sw_ref[g], 0.0).astype(BF16)
        bias = sb_ref[:, g:g + 1]
        cs = slice(g * SGU_GROUP_DIM, (g + 1) * SGU_GROUP_DIM)
        for c in range(tm // SGU_CHUNK):
            rs = slice(c * SGU_CHUNK, (c + 1) * SGU_CHUNK)
            mixed = jnp.dot(wg, va[rs, cs], preferred_element_type=F32) + bias
            a_ref[rs, cs] = (u[rs, cs] * mixed).astype(a_ref.dtype)

    q = _rope(jnp.dot(hb, wm_ref[:, OFF_Q:OFF_K], preferred_element_type=F32),
              tab_c, tab_s1, tab_s2) * (ATT_HEAD_DIM ** -0.5)
    k = _rope(jnp.dot(hb, wm_ref[:, OFF_K:OFF_V], preferred_element_type=F32),
              tab_c, tab_s1, tab_s2)
    v = jnp.dot(hb, wm_ref[:, OFF_V:OFF_IQ], preferred_element_type=F32)
    for hd in range(ATT_HEADS):
        hs = slice(hd * ATT_HEAD_DIM, (hd + 1) * ATT_HEAD_DIM)
        q_ref[hd] = q[:, hs].astype(q_ref.dtype)
        k_ref[hd] = k[:, hs].astype(k_ref.dtype)
        v_ref[hd] = v[:, hs].astype(v_ref.dtype)

    whi, wlo = _split_bf16(wi_ref[...])
    pi = (jnp.dot(hb, whi, preferred_element_type=F32)
          + jnp.dot(hb, wlo, preferred_element_type=F32)
          + jnp.dot(hlo, whi, preferred_element_type=F32))
    n_iq = IDX_HEADS * IDX_HEAD_DIM
    qi = _rope(pi[:, :n_iq], tab_c, tab_s1, tab_s2)
    lane = lax.broadcasted_iota(I32, (tm, LANES), 1)
    is_ki = lane < IDX_HEAD_DIM
    tail = _rope(pi[:, n_iq:], jnp.where(is_ki, tab_c, 1.0),
                 jnp.where(is_ki, tab_s1, 0.0), jnp.where(is_ki, tab_s2, 0.0))
    for hd in range(IDX_HEADS):
        hi, lo = _split_bf16(qi[:, hd * IDX_HEAD_DIM:(hd + 1) * IDX_HEAD_DIM])
        qi_ref[hd] = jnp.concatenate([hi, hi, lo, lo], axis=1)
    hi, lo = _split_bf16(tail[:, :IDX_HEAD_DIM])
    ki_ref[...] = jnp.concatenate([hi, lo, hi, lo], axis=1)
    wiT_ref[...] = tail.T[IDX_HEAD_DIM:IDX_HEAD_DIM + IDX_HEADS, :]


def _mix_call(x, mod, layer, norm_w, w_main, w_idx, sgu_w, sgu_bT, rope_tab):
    b, s, d = x.shape
    tm = min(TOKEN_BLOCK, s)
    tok = lambda bi, i: (bi, i, 0)
    head_major = lambda bi, i: (bi, 0, i, 0)
    out_shape = [
        jax.ShapeDtypeStruct((b, s, SGU_WIDTH), BF16),
        jax.ShapeDtypeStruct((b, ATT_HEADS, s, ATT_HEAD_DIM), BF16),
        jax.ShapeDtypeStruct((b, ATT_HEADS, s, ATT_HEAD_DIM), BF16),
        jax.ShapeDtypeStruct((b, ATT_HEADS, s, ATT_HEAD_DIM), BF16),
        jax.ShapeDtypeStruct((b, IDX_HEADS, s, 4 * IDX_HEAD_DIM), BF16),
        jax.ShapeDtypeStruct((b, s, 4 * IDX_HEAD_DIM), BF16),
        jax.ShapeDtypeStruct((b, IDX_HEADS, s), F32),
    ]
    out_specs = [
        pl.BlockSpec((None, tm, SGU_WIDTH), tok),
        pl.BlockSpec((None, ATT_HEADS, tm, ATT_HEAD_DIM), head_major),
        pl.BlockSpec((None, ATT_HEADS, tm, ATT_HEAD_DIM), head_major),
        pl.BlockSpec((None, ATT_HEADS, tm, ATT_HEAD_DIM), head_major),
        pl.BlockSpec((None, IDX_HEADS, tm, 4 * IDX_HEAD_DIM), head_major),
        pl.BlockSpec((None, tm, 4 * IDX_HEAD_DIM), tok),
        pl.BlockSpec((None, IDX_HEADS, tm), lambda bi, i: (bi, 0, i)),
    ]
    in_specs = [
        pl.BlockSpec((None, tm, d), tok),
        pl.BlockSpec((None, None, N_MOD, d), lambda bi, i: (layer, bi, 0, 0)),
        pl.BlockSpec((None, 1, d), lambda bi, i: (layer, 0, 0)),
        _resident((None, d, OFF_IQ), lambda bi, i: (layer, 0, 0)),
        _resident((None, d, IDX_PAD), lambda bi, i: (layer, 0, 0)),
        _resident((None, SGU_GROUPS, SGU_CHUNK, SGU_CHUNK), lambda bi, i: (layer, 0, 0, 0)),
        _resident((None, SGU_CHUNK, SGU_GROUPS), lambda bi, i: (layer, 0, 0)),
        pl.BlockSpec((None, 3, tm, LANES), lambda bi, i: (bi, 0, i, 0)),
    ]
    return pl.pallas_call(
        _mix_kernel,
        grid=(b, s // tm),
        in_specs=in_specs,
        out_specs=out_specs,
        out_shape=out_shape,
        compiler_params=pltpu.CompilerParams(
            dimension_semantics=("parallel", "parallel"),
            vmem_limit_bytes=VMEM_LIMIT),
        name="mix_proj",
    )(x, mod, norm_w, w_main, w_idx, sgu_w, sgu_bT, rope_tab)


def _attn_kernel(q_ref, k_ref, v_ref, qi_ref, ki_ref, wiT_ref, o_ref,
                 keys_ref, bias_ref, m_ref, l_ref, acc_ref, *, topk, seq_bits):
    tq = q_ref.shape[1]
    kc = tq
    j = pl.program_id(1)
    n_tiles = j + 1
    q_start = j * tq
    nt_dims = (((1,), (1,)), ((), ()))

    qi_all = qi_ref[...].reshape(IDX_HEADS * tq, qi_ref.shape[-1])
    wi = wiT_ref[...]
    q_idx = q_start + lax.broadcasted_iota(I32, (kc, tq), 1)
    k_iota = lax.broadcasted_iota(I32, (kc, tq), 0)

    def score_tile(kt, carry):
        k0 = pl.multiple_of(kt * kc, kc)
        logits = lax.dot_general(ki_ref[pl.ds(k0, kc), :], qi_all, nt_dims,
                                 preferred_element_type=F32)
        sc = wi[0:1, :] * jnp.maximum(logits[:, 0:tq], 0.0)
        for hd in range(1, IDX_HEADS):
            sc = sc + wi[hd:hd + 1, :] * jnp.maximum(logits[:, hd * tq:(hd + 1) * tq], 0.0)
        sc = sc + 0.0
        bits = lax.bitcast_convert_type(sc, I32)
        key = bits ^ ((bits >> 31) & 0x7FFFFFFF)
        keys_ref[kt] = jnp.where(k0 + k_iota <= q_idx, key, INT_MIN)
        return carry

    lax.fori_loop(0, n_tiles, score_tile, 0)

    def count(pred):
        def body(kt, acc):
            ind = jnp.where(pred(keys_ref[kt], kt * kc + k_iota), 1, 0)
            return acc + jnp.sum(ind.reshape(kc // 8, 8, tq), axis=0)
        acc = lax.fori_loop(0, n_tiles, body, jnp.zeros((8, tq), I32))
        return jnp.sum(acc, axis=0, keepdims=True)

    def bisect(i, t):
        cand = t | jnp.left_shift(jnp.int32(1), 31 - i)
        cand_s = cand ^ INT_MIN
        cnt = count(lambda kk, _: kk >= cand_s)
        return jnp.where(cnt >= topk, cand, t)

    thr = lax.fori_loop(0, 32, bisect, jnp.zeros((1, tq), I32)) ^ INT_MIN

    n_gt = count(lambda kk, _: kk > thr)
    n_eq = count(lambda kk, _: kk == thr)
    need = topk - n_gt
    real = thr != INT_MIN
    tie_rows = jnp.where(real & (n_eq > need), 1, 0)

    def resolve_ties():
        def step(i, p):
            cand = p | jnp.left_shift(jnp.int32(1), seq_bits - 1 - i)
            cnt = count(lambda kk, ki: (kk == thr) & (ki < cand))
            return jnp.where(cnt < need, cand, p)
        return lax.fori_loop(0, seq_bits, step, jnp.zeros((1, tq), I32))

    last_tie = lax.cond(jnp.max(tie_rows) > 0, resolve_ties,
                        lambda: jnp.full((1, tq), (1 << seq_bits) - 1, I32))
    last_tie = jnp.where(real, last_tie, -1)

    def bias_tile(kt, carry):
        kk = keys_ref[kt]
        sel = (kk > thr) | ((kk == thr) & (kt * kc + k_iota <= last_tie))
        bias_ref[kt] = jnp.where(sel, 0.0, NEG_BIAS).T
        return carry

    lax.fori_loop(0, n_tiles, bias_tile, 0)

    m_ref[...] = jnp.full_like(m_ref, -jnp.inf)
    l_ref[...] = jnp.zeros_like(l_ref)
    acc_ref[...] = jnp.zeros_like(acc_ref)

    def attend(kt, carry):
        k0 = pl.multiple_of(kt * kc, kc)
        bias = bias_ref[kt]
        for hd in range(ATT_HEADS):
            s = lax.dot_general(q_ref[hd], k_ref[hd, pl.ds(k0, kc), :], nt_dims,
                                preferred_element_type=F32) + bias
            m_prev = m_ref[hd]
            m_new = jnp.maximum(m_prev, jnp.max(s, axis=1, keepdims=True))
            alpha = jnp.exp(m_prev - m_new)
            p = jnp.exp(s - jnp.concatenate([m_new] * (kc // LANES), axis=1))
            l_ref[hd] = alpha * l_ref[hd] + jnp.sum(p, axis=1, keepdims=True)
            pv = jnp.dot(p.astype(BF16), v_ref[hd, pl.ds(k0, kc), :],
                         preferred_element_type=F32)
            acc_ref[hd] = acc_ref[hd] * alpha[:, :ATT_HEAD_DIM] + pv
            m_ref[hd] = m_new
        return carry

    lax.fori_loop(0, n_tiles, attend, 0)

    outs = [acc_ref[hd] / l_ref[hd][:, :ATT_HEAD_DIM] for hd in range(ATT_HEADS)]
    o_ref[...] = jnp.concatenate(outs, axis=1).astype(o_ref.dtype)


def _attn_call(q, k, v, qi, ki, wiT):
    b, n_head, s, hd = q.shape
    tq = min(ATT_BLOCK, s)
    n_blk = s // tq
    topk = min(TOPK_MAX, s // 4)
    seq_bits = max(1, (s - 1).bit_length())
    blk = lambda bi, j: (bi, 0, j, 0)
    whole = lambda bi, j: (bi, 0, 0, 0)
    return pl.pallas_call(
        functools.partial(_attn_kernel, topk=topk, seq_bits=seq_bits),
        grid=(b, n_blk),
        in_specs=[
            pl.BlockSpec((None, n_head, tq, hd), blk),
            pl.BlockSpec((None, n_head, s, hd), whole),
            pl.BlockSpec((None, n_head, s, hd), whole),
            pl.BlockSpec((None, IDX_HEADS, tq, qi.shape[-1]), blk),
            pl.BlockSpec((None, s, ki.shape[-1]), lambda bi, j: (bi, 0, 0)),
            pl.BlockSpec((None, IDX_HEADS, tq), lambda bi, j: (bi, 0, j)),
        ],
        out_specs=pl.BlockSpec((None, tq, n_head * hd), lambda bi, j: (bi, j, 0)),
        out_shape=jax.ShapeDtypeStruct((b, s, n_head * hd), BF16),
        scratch_shapes=[
            pltpu.VMEM((n_blk, tq, tq), I32),
            pltpu.VMEM((n_blk, tq, tq), F32),
            pltpu.VMEM((n_head, tq, LANES), F32),
            pltpu.VMEM((n_head, tq, LANES), F32),
            pltpu.VMEM((n_head, tq, hd), F32),
        ],
        compiler_params=pltpu.CompilerParams(
            dimension_semantics=("parallel", "arbitrary"),
            vmem_limit_bytes=VMEM_LIMIT),
        name="dsa_attention",
    )(q, k, v, qi, ki, wiT)


def _rope_tables(positions):
    inv_freq = ROPE_THETA ** (-jnp.arange(0, ROPE_DIM, 2, dtype=F32) / ROPE_DIM)
    ang = positions.astype(F32)[..., None] * inv_freq
    cos, sin = jnp.cos(ang), jnp.sin(ang)
    rest = ATT_HEAD_DIM - ROPE_DIM
    ones = jnp.ones(cos.shape[:-1] + (rest,), F32)
    zeros = jnp.zeros(cos.shape[:-1] + (rest,), F32)
    zhalf = jnp.zeros_like(sin)
    tab_c = jnp.concatenate([cos, cos, ones], axis=-1)
    tab_s1 = jnp.concatenate([-sin, zhalf, zeros], axis=-1)
    tab_s2 = jnp.concatenate([zhalf, sin, zeros], axis=-1)
    tabs = jnp.stack([tab_c, tab_s1, tab_s2], axis=1)
    return jnp.concatenate([tabs] * (LANES // ATT_HEAD_DIM), axis=-1)


def _ffn_weights(w_in, w_out):
    n_layer, d, two_ff = w_in.shape
    d_ff = two_ff // 2
    n = d_ff // FF_CHUNK
    wg = w_in[:, :, :d_ff].reshape(n_layer, d, n, FF_CHUNK)
    wu = w_in[:, :, d_ff:].reshape(n_layer, d, n, FF_CHUNK)
    w_in_r = jnp.concatenate([wg, wu], axis=-1).transpose(0, 2, 1, 3).astype(BF16)
    w_out_r = w_out.reshape(n_layer, n, FF_CHUNK, d).astype(BF16)
    return w_in_r, w_out_r


def kernel(x, c, positions, ada_w, ada_b, norm_ffn1, ffn1_w_in, ffn1_w_out, norm_mix, mix_w_in,
           sgu_w, sgu_b, mix_w_out, norm_ffn2, ffn2_w_in, ffn2_w_out, final_norm):
    n_layer, d = norm_ffn1.shape
    b = x.shape[0]

    mod = _ada_call(c, ada_w, ada_b).reshape(n_layer, b, N_MOD, d)
    rope_tab = _rope_tables(positions)

    f1_in, f1_out = _ffn_weights(ffn1_w_in, ffn1_w_out)
    f2_in, f2_out = _ffn_weights(ffn2_w_in, ffn2_w_out)
    w_main = mix_w_in[:, :, :OFF_IQ].astype(BF16)
    w_idx = jnp.pad(mix_w_in[:, :, OFF_IQ:], ((0, 0), (0, 0), (0, IDX_PAD - (PROJ_WIDTH - OFF_IQ))))
    w_mix_out = mix_w_out.astype(BF16)
    sgu_bT = sgu_b.transpose(0, 2, 1)
    n1 = norm_ffn1.reshape(n_layer, 1, d)
    nm = norm_mix.reshape(n_layer, 1, d)
    n2 = norm_ffn2.reshape(n_layer, 1, d)
    fw = final_norm.reshape(1, d)

    for l in range(n_layer):
        x = _ffn_call(x, mod, l, 0, n1, f1_in, f1_out)
        a_out, q, k, v, qi, ki, wiT = _mix_call(x, mod, l, nm, w_main, w_idx, sgu_w, sgu_bT, rope_tab)
        b_out = _attn_call(q, k, v, qi, ki, wiT)
        x = _ffn_call(x, mod, l, 6, n2, f2_in, f2_out, mix=(a_out, b_out, w_mix_out),
                      final_w=fw if l == n_layer - 1 else None)
    return x
```

```python
import functools

import jax
import jax.numpy as jnp
from jax import lax
from jax.experimental import pallas as pl
from jax.experimental.pallas import tpu as pltpu

F32 = jnp.float32
BF16 = jnp.bfloat16
I32 = jnp.int32

SGU_CHUNK = 128
SGU_GROUPS = 4
SGU_GROUP_DIM = 128
SGU_WIDTH = SGU_GROUPS * SGU_GROUP_DIM
ATT_HEAD_DIM = 64
ATT_HEADS = 8
ATT_WIDTH = ATT_HEADS * ATT_HEAD_DIM
IDX_HEADS = 4
IDX_HEAD_DIM = 64
TOPK_MAX = 256
ROPE_THETA = 500000.0
ROPE_DIM = ATT_HEAD_DIM // 4
ROPE_HALF = ROPE_DIM // 2
RMS_EPS = 1e-6
N_MOD = 9

OFF_SGU_U = 0
OFF_SGU_V = OFF_SGU_U + SGU_WIDTH
OFF_Q = OFF_SGU_V + SGU_WIDTH
OFF_K = OFF_Q + ATT_WIDTH
OFF_V = OFF_K + ATT_WIDTH
OFF_IQ = OFF_V + ATT_WIDTH
OFF_IK = OFF_IQ + IDX_HEADS * IDX_HEAD_DIM
OFF_IW = OFF_IK + IDX_HEAD_DIM
PROJ_WIDTH = OFF_IW + IDX_HEADS

LANES = 128
SUBLANES = 8
FF_CHUNK = 256
TOKEN_BLOCK = 512
ATT_BLOCK = 256
IDX_PAD = 384
COUNT_CHAINS = 4
VMEM_LIMIT = 52 * 1024 * 1024

INT_MIN = -2147483648
NEG_BIAS = -1e30


def _resident(block_shape, index_map):
    return pl.BlockSpec(block_shape, index_map, pipeline_mode=pl.Buffered(1))


def _rms_norm(x, w):
    ms = jnp.mean(x * x, axis=-1, keepdims=True)
    return x * lax.rsqrt(ms + RMS_EPS) * w


def _gelu(x):
    return 0.5 * x * (1.0 + lax.erf(x * (0.5 ** 0.5)))


def _split_bf16(x):
    hi = x.astype(BF16)
    lo = (x - hi.astype(F32)).astype(BF16)
    return hi, lo


def _ada_kernel(c_ref, w_ref, b_ref, o_ref):
    ca = jax.nn.silu(c_ref[...]).astype(BF16)
    w = w_ref[...].astype(BF16)
    o_ref[...] = jnp.dot(ca, w, preferred_element_type=F32) + b_ref[...]


def _ada_call(c, ada_w, ada_b):
    n_layer, d, n_out = ada_w.shape
    b = c.shape[0]
    tn = n_out // 4
    return pl.pallas_call(
        _ada_kernel,
        grid=(n_layer, n_out // tn),
        in_specs=[
            pl.BlockSpec((b, d), lambda l, n: (0, 0)),
            pl.BlockSpec((None, d, tn), lambda l, n: (l, 0, n)),
            pl.BlockSpec((None, 1, tn), lambda l, n: (l, 0, n)),
        ],
        out_specs=pl.BlockSpec((None, b, tn), lambda l, n: (l, 0, n)),
        out_shape=jax.ShapeDtypeStruct((n_layer, b, n_out), F32),
        compiler_params=pltpu.CompilerParams(
            dimension_semantics=("arbitrary", "arbitrary"),
            vmem_limit_bytes=VMEM_LIMIT),
        name="ada_mod",
    )(c, ada_w, ada_b.reshape(n_layer, 1, n_out))


def _ffn_kernel(*refs, mod_row, has_mix, has_final):
    it = iter(refs)
    x_ref, mod_ref, nw_ref, win_ref, wout_ref = (next(it) for _ in range(5))
    if has_mix:
        a_ref, b_ref, wmix_ref = (next(it) for _ in range(3))
    if has_final:
        fw_ref = next(it)
    o_ref = next(it)
    acc_ref = next(it)
    d_ff = wout_ref.shape[0]

    x = x_ref[...]
    if has_mix:
        half = a_ref.shape[-1]
        y = jnp.dot(a_ref[...], wmix_ref[0:half, :], preferred_element_type=F32)
        y = y + jnp.dot(b_ref[...], wmix_ref[half:, :], preferred_element_type=F32)
        x = x + mod_ref[5:6, :] * y

    shift = mod_ref[mod_row:mod_row + 1, :]
    scale = mod_ref[mod_row + 1:mod_row + 2, :]
    gate = mod_ref[mod_row + 2:mod_row + 3, :]
    h = _rms_norm(x, nw_ref[...]) * (1.0 + scale) + shift
    hb = h.astype(BF16)

    for ci in range(d_ff // FF_CHUNK):
        cs = slice(ci * FF_CHUNK, (ci + 1) * FF_CHUNK)
        us = slice(d_ff + ci * FF_CHUNK, d_ff + (ci + 1) * FF_CHUNK)
        g = jnp.dot(hb, win_ref[:, cs], preferred_element_type=F32)
        u = jnp.dot(hb, win_ref[:, us], preferred_element_type=F32)
        a = (g * jax.nn.sigmoid(g) * u).astype(BF16)
        part = jnp.dot(a, wout_ref[cs, :], preferred_element_type=F32)
        if ci == 0:
            acc_ref[...] = part
        else:
            acc_ref[...] += part

    out = x + 0.5 * gate * acc_ref[...]
    if has_final:
        out = _rms_norm(out, fw_ref[...])
    o_ref[...] = out


def _ffn_call(x, mod, layer, mod_row, norm_w, w_in, w_out, mix=None, final_w=None):
    b, s, d = x.shape
    tm = min(TOKEN_BLOCK, s)
    d_ff = w_out.shape[1]
    has_mix = mix is not None
    has_final = final_w is not None

    tok = lambda bi, i: (bi, i, 0)
    in_specs = [
        pl.BlockSpec((None, tm, d), tok),
        pl.BlockSpec((None, None, N_MOD, d), lambda bi, i: (layer, bi, 0, 0)),
        pl.BlockSpec((None, 1, d), lambda bi, i: (layer, 0, 0)),
        _resident((None, d, 2 * d_ff), lambda bi, i: (layer, 0, 0)),
        _resident((None, d_ff, d), lambda bi, i: (layer, 0, 0)),
    ]
    args = [x, mod, norm_w, w_in, w_out]
    if has_mix:
        a_out, b_out, w_mix = mix
        in_specs += [
            pl.BlockSpec((None, tm, a_out.shape[-1]), tok),
            pl.BlockSpec((None, tm, b_out.shape[-1]), tok),
            _resident((None, w_mix.shape[1], d), lambda bi, i: (layer, 0, 0)),
        ]
        args += [a_out, b_out, w_mix]
    if has_final:
        in_specs.append(pl.BlockSpec((1, d), lambda bi, i: (0, 0)))
        args.append(final_w)

    return pl.pallas_call(
        functools.partial(_ffn_kernel, mod_row=mod_row, has_mix=has_mix, has_final=has_final),
        grid=(b, s // tm),
        in_specs=in_specs,
        out_specs=pl.BlockSpec((None, tm, d), tok),
        out_shape=jax.ShapeDtypeStruct((b, s, d), F32),
        scratch_shapes=[pltpu.VMEM((tm, d), F32)],
        compiler_params=pltpu.CompilerParams(
            dimension_semantics=("parallel", "parallel"),
            vmem_limit_bytes=VMEM_LIMIT),
        name="ffn_mix" if has_mix else "ffn",
    )(*args)


def _rope(p, tab_c, tab_s1, tab_s2):
    width = p.shape[-1]
    reps = width // LANES
    if reps > 1:
        tab_c = jnp.concatenate([tab_c] * reps, axis=1)
        tab_s1 = jnp.concatenate([tab_s1] * reps, axis=1)
        tab_s2 = jnp.concatenate([tab_s2] * reps, axis=1)
    up = pltpu.roll(p, width - ROPE_HALF, 1)
    down = pltpu.roll(p, ROPE_HALF, 1)
    return p * tab_c + up * tab_s1 + down * tab_s2


def _mix_kernel(x_ref, mod_ref, nw_ref, wm_ref, wi_ref, sw_ref, sb_ref, rope_ref,
                a_ref, q_ref, k_ref, v_ref, qi_ref, ki_ref, wiT_ref):
    tm = x_ref.shape[0]
    x = x_ref[...]
    h = _rms_norm(x, nw_ref[...]) * (1.0 + mod_ref[4:5, :]) + mod_ref[3:4, :]
    hb, hlo = _split_bf16(h)

    tab_c = rope_ref[0]
    tab_s1 = rope_ref[1]
    tab_s2 = rope_ref[2]

    u = _gelu(jnp.dot(hb, wm_ref[:, OFF_SGU_U:OFF_SGU_V], preferred_element_type=F32))
    va = _gelu(jnp.dot(hb, wm_ref[:, OFF_SGU_V:OFF_Q], preferred_element_type=F32)).astype(BF16)
    row = lax.broadcasted_iota(I32, (SGU_CHUNK, SGU_CHUNK), 0)
    col = lax.broadcasted_iota(I32, (SGU_CHUNK, SGU_CHUNK), 1)
    causal = col <= row
    for g in range(SGU_GROUPS):
        wg = jnp.where(causal, sw_ref[g], 0.0).astype(BF16)
        bias = sb_ref[:, g:g + 1]
        cs = slice(g * SGU_GROUP_DIM, (g + 1) * SGU_GROUP_DIM)
        for c in range(tm // SGU_CHUNK):
            rs = slice(c * SGU_CHUNK, (c + 1) * SGU_CHUNK)
            mixed = jnp.dot(wg, va[rs, cs], preferred_element_type=F32) + bias
            a_ref[rs, cs] = (u[rs, cs] * mixed).astype(a_ref.dtype)

    q = _rope(jnp.dot(hb, wm_ref[:, OFF_Q:OFF_K], preferred_element_type=F32),
              tab_c, tab_s1, tab_s2) * (ATT_HEAD_DIM ** -0.5)
    k = _rope(jnp.dot(hb, wm_ref[:, OFF_K:OFF_V], preferred_element_type=F32),
              tab_c, tab_s1, tab_s2)
    v = jnp.dot(hb, wm_ref[:, OFF_V:OFF_IQ], preferred_element_type=F32)
    ones = jnp.ones((tm, ATT_HEAD_DIM), F32)
    for hd in range(ATT_HEADS):
        hs = slice(hd * ATT_HEAD_DIM, (hd + 1) * ATT_HEAD_DIM)
        q_ref[hd] = q[:, hs].astype(q_ref.dtype)
        k_ref[hd] = k[:, hs].astype(k_ref.dtype)
        v_ref[hd] = jnp.concatenate([v[:, hs], ones], axis=1).astype(v_ref.dtype)

    whi, wlo = _split_bf16(wi_ref[...])
    pi = (jnp.dot(hb, whi, preferred_element_type=F32)
          + jnp.dot(hb, wlo, preferred_element_type=F32)
          + jnp.dot(hlo, whi, preferred_element_type=F32))
    n_iq = IDX_HEADS * IDX_HEAD_DIM
    qi = _rope(pi[:, :n_iq], tab_c, tab_s1, tab_s2)
    lane = lax.broadcasted_iota(I32, (tm, LANES), 1)
    is_ki = lane < IDX_HEAD_DIM
    tail = _rope(pi[:, n_iq:], jnp.where(is_ki, tab_c, 1.0),
                 jnp.where(is_ki, tab_s1, 0.0), jnp.where(is_ki, tab_s2, 0.0))
    for hd in range(IDX_HEADS):
        hi, lo = _split_bf16(qi[:, hd * IDX_HEAD_DIM:(hd + 1) * IDX_HEAD_DIM])
        qi_ref[hd] = jnp.concatenate([hi, hi, lo, lo], axis=1)
    hi, lo = _split_bf16(tail[:, :IDX_HEAD_DIM])
    ki_ref[...] = jnp.concatenate([hi, lo, hi, lo], axis=1)
    wiT_ref[...] = tail.T[IDX_HEAD_DIM:IDX_HEAD_DIM + IDX_HEADS, :]


def _mix_call(x, mod, layer, norm_w, w_main, w_idx, sgu_w, sgu_bT, rope_tab):
    b, s, d = x.shape
    tm = min(TOKEN_BLOCK, s)
    tok = lambda bi, i: (bi, i, 0)
    head_major = lambda bi, i: (bi, 0, i, 0)
    out_shape = [
        jax.ShapeDtypeStruct((b, s, SGU_WIDTH), BF16),
        jax.ShapeDtypeStruct((b, ATT_HEADS, s, ATT_HEAD_DIM), BF16),
        jax.ShapeDtypeStruct((b, ATT_HEADS, s, ATT_HEAD_DIM), BF16),
        jax.ShapeDtypeStruct((b, ATT_HEADS, s, 2 * ATT_HEAD_DIM), BF16),
        jax.ShapeDtypeStruct((b, IDX_HEADS, s, 4 * IDX_HEAD_DIM), BF16),
        jax.ShapeDtypeStruct((b, s, 4 * IDX_HEAD_DIM), BF16),
        jax.ShapeDtypeStruct((b, IDX_HEADS, s), F32),
    ]
    out_specs = [
        pl.BlockSpec((None, tm, SGU_WIDTH), tok),
        pl.BlockSpec((None, ATT_HEADS, tm, ATT_HEAD_DIM), head_major),
        pl.BlockSpec((None, ATT_HEADS, tm, ATT_HEAD_DIM), head_major),
        pl.BlockSpec((None, ATT_HEADS, tm, 2 * ATT_HEAD_DIM), head_major),
        pl.BlockSpec((None, IDX_HEADS, tm, 4 * IDX_HEAD_DIM), head_major),
        pl.BlockSpec((None, tm, 4 * IDX_HEAD_DIM), tok),
        pl.BlockSpec((None, IDX_HEADS, tm), lambda bi, i: (bi, 0, i)),
    ]
    in_specs = [
        pl.BlockSpec((None, tm, d), tok),
        pl.BlockSpec((None, None, N_MOD, d), lambda bi, i: (layer, bi, 0, 0)),
        pl.BlockSpec((None, 1, d), lambda bi, i: (layer, 0, 0)),
        _resident((None, d, OFF_IQ), lambda bi, i: (layer, 0, 0)),
        _resident((None, d, IDX_PAD), lambda bi, i: (layer, 0, 0)),
        _resident((None, SGU_GROUPS, SGU_CHUNK, SGU_CHUNK), lambda bi, i: (layer, 0, 0, 0)),
        _resident((None, SGU_CHUNK, SGU_GROUPS), lambda bi, i: (layer, 0, 0)),
        pl.BlockSpec((None, 3, tm, LANES), lambda bi, i: (bi, 0, i, 0)),
    ]
    return pl.pallas_call(
        _mix_kernel,
        grid=(b, s // tm),
        in_specs=in_specs,
        out_specs=out_specs,
        out_shape=out_shape,
        compiler_params=pltpu.CompilerParams(
            dimension_semantics=("parallel", "parallel"),
            vmem_limit_bytes=VMEM_LIMIT),
        name="mix_proj",
    )(x, mod, norm_w, w_main, w_idx, sgu_w, sgu_bT, rope_tab)


def _ordered_bits_to_float(u):
    key = u ^ INT_MIN
    bits = key ^ ((key >> 31) & 0x7FFFFFFF)
    return lax.bitcast_convert_type(bits, F32)


def _attn_kernel(q_ref, k_ref, v_ref, qi_ref, ki_ref, wiT_ref, o_ref,
                 sc_ref, bias_ref, m_ref, acc_ref, *, topk, seq_bits):
    tq = q_ref.shape[1]
    kc = tq
    n_head, hd = q_ref.shape[0], q_ref.shape[2]
    j = pl.program_id(1)
    n_tiles = j + 1
    q_start = j * tq
    nt_dims = (((1,), (1,)), ((), ()))

    qi_all = qi_ref[...].reshape(IDX_HEADS * tq, qi_ref.shape[-1])
    wi = wiT_ref[...]
    q_idx = q_start + lax.broadcasted_iota(I32, (kc, tq), 1)
    k_iota = lax.broadcasted_iota(I32, (kc, tq), 0)
    full = q_start + lax.broadcasted_iota(I32, (1, tq), 1) + 1 >= topk

    def select_keys(n):
        for kt in range(n):
            logits = lax.dot_general(ki_ref[kt * kc:(kt + 1) * kc, :], qi_all, nt_dims,
                                     preferred_element_type=F32)
            sc = wi[0:1, :] * jnp.maximum(logits[:, 0:tq], 0.0)
            for h in range(1, IDX_HEADS):
                sc = sc + wi[h:h + 1, :] * jnp.maximum(logits[:, h * tq:(h + 1) * tq], 0.0)
            sc = sc + 0.0
            if kt == n - 1:
                sc = jnp.where(kt * kc + k_iota <= q_idx, sc, -jnp.inf)
            sc_ref[kt] = sc

        def count(pred):
            acc = jnp.zeros((COUNT_CHAINS, SUBLANES, tq), I32)
            for kt in range(n):
                ind = jnp.where(pred(sc_ref[kt], kt * kc + k_iota), 1, 0)
                acc = acc + jnp.sum(ind.reshape(COUNT_CHAINS, -1, SUBLANES, tq), axis=1)
            return jnp.sum(acc.reshape(-1, tq), axis=0, keepdims=True)

        def bisect(i, t):
            cand = t | jnp.left_shift(jnp.int32(1), 31 - i)
            cand_f = _ordered_bits_to_float(cand)
            cnt = count(lambda s, _: s >= cand_f)
            return jnp.where(cnt >= topk, cand, t)

        t_bits = lax.fori_loop(0, 32, bisect, jnp.zeros((1, tq), I32))
        thr = jnp.where(full, _ordered_bits_to_float(t_bits), -jnp.inf)

        n_gt = count(lambda s, _: s > thr)
        n_eq = count(lambda s, _: s == thr)
        need = topk - n_gt
        tie_rows = jnp.where(full & (n_eq > need), 1, 0)

        def resolve_ties():
            def step(i, p):
                cand = p | jnp.left_shift(jnp.int32(1), seq_bits - 1 - i)
                cnt = count(lambda s, ki: (s == thr) & (ki < cand))
                return jnp.where(cnt < need, cand, p)
            return lax.fori_loop(0, seq_bits, step, jnp.zeros((1, tq), I32))

        last_tie = lax.cond(jnp.max(tie_rows) > 0, resolve_ties,
                            lambda: jnp.full((1, tq), (1 << seq_bits) - 1, I32))
        last_tie = jnp.where(full, last_tie, -1)

        for kt in range(n):
            s = sc_ref[kt]
            sel = (s > thr) | ((s == thr) & (kt * kc + k_iota <= last_tie))
            bias_ref[kt] = jnp.where(sel, 0.0, NEG_BIAS).T
        if n % 2 == 1 and n < bias_ref.shape[0]:
            bias_ref[n] = jnp.full((tq, kc), NEG_BIAS, F32)
        return 0

    lax.switch(j, [functools.partial(select_keys, n + 1) for n in range(bias_ref.shape[0])])

    m_ref[...] = jnp.full_like(m_ref, -jnp.inf)
    acc_ref[...] = jnp.zeros_like(acc_ref)
    pair = min(2, bias_ref.shape[0])

    def attend(kp, carry):
        for t in range(pair):
            kt = pair * kp + t
            k0 = pl.multiple_of(kt * kc, kc)
            bias = bias_ref[kt]
            for h in range(n_head):
                s = lax.dot_general(q_ref[h], k_ref[h, pl.ds(k0, kc), :], nt_dims,
                                    preferred_element_type=F32) + bias
                m_prev = m_ref[h]
                m_new = jnp.maximum(m_prev, jnp.max(s, axis=1, keepdims=True))
                alpha = jnp.exp(m_prev - m_new)
                p = jnp.exp(s - jnp.concatenate([m_new] * (kc // LANES), axis=1))
                pv = jnp.dot(p.astype(BF16), v_ref[h, pl.ds(k0, kc), :],
                             preferred_element_type=F32)
                acc_ref[h] = acc_ref[h] * alpha + pv
                m_ref[h] = m_new
        return carry

    lax.fori_loop(0, (n_tiles + pair - 1) // pair, attend, 0)

    outs = []
    for h in range(n_head):
        acc = acc_ref[h]
        outs.append(acc[:, :hd] / acc[:, hd:])
    o_ref[...] = jnp.concatenate(outs, axis=1).astype(o_ref.dtype)


def _attn_call(q, k, v, qi, ki, wiT):
    b, n_head, s, hd = q.shape
    tq = min(ATT_BLOCK, s)
    n_blk = s // tq
    topk = min(TOPK_MAX, s // 4)
    seq_bits = max(1, (s - 1).bit_length())
    blk = lambda bi, j: (bi, 0, j, 0)
    whole = lambda bi, j: (bi, 0, 0, 0)
    return pl.pallas_call(
        functools.partial(_attn_kernel, topk=topk, seq_bits=seq_bits),
        grid=(b, n_blk),
        in_specs=[
            pl.BlockSpec((None, n_head, tq, hd), blk),
            pl.BlockSpec((None, n_head, s, hd), whole),
            pl.BlockSpec((None, n_head, s, v.shape[-1]), whole),
            pl.BlockSpec((None, IDX_HEADS, tq, qi.shape[-1]), blk),
            pl.BlockSpec((None, s, ki.shape[-1]), lambda bi, j: (bi, 0, 0)),
            pl.BlockSpec((None, IDX_HEADS, tq), lambda bi, j: (bi, 0, j)),
        ],
        out_specs=pl.BlockSpec((None, tq, n_head * hd), lambda bi, j: (bi, j, 0)),
        out_shape=jax.ShapeDtypeStruct((b, s, n_head * hd), BF16),
        scratch_shapes=[
            pltpu.VMEM((n_blk, tq, tq), F32),
            pltpu.VMEM((n_blk, tq, tq), F32),
            pltpu.VMEM((n_head, tq, LANES), F32),
            pltpu.VMEM((n_head, tq, v.shape[-1]), F32),
        ],
        compiler_params=pltpu.CompilerParams(
            dimension_semantics=("parallel", "arbitrary"),
            vmem_limit_bytes=VMEM_LIMIT),
        name="dsa_attention",
    )(q, k, v, qi, ki, wiT)


def _rope_tables(positions):
    inv_freq = ROPE_THETA ** (-jnp.arange(0, ROPE_DIM, 2, dtype=F32) / ROPE_DIM)
    ang = positions.astype(F32)[..., None] * inv_freq
    cos, sin = jnp.cos(ang), jnp.sin(ang)
    rest = ATT_HEAD_DIM - ROPE_DIM
    ones = jnp.ones(cos.shape[:-1] + (rest,), F32)
    zeros = jnp.zeros(cos.shape[:-1] + (rest,), F32)
    zhalf = jnp.zeros_like(sin)
    tab_c = jnp.concatenate([cos, cos, ones], axis=-1)
    tab_s1 = jnp.concatenate([-sin, zhalf, zeros], axis=-1)
    tab_s2 = jnp.concatenate([zhalf, sin, zeros], axis=-1)
    tabs = jnp.stack([tab_c, tab_s1, tab_s2], axis=1)
    return jnp.concatenate([tabs] * (LANES // ATT_HEAD_DIM), axis=-1)


def kernel(x, c, positions, ada_w, ada_b, norm_ffn1, ffn1_w_in, ffn1_w_out, norm_mix, mix_w_in,
           sgu_w, sgu_b, mix_w_out, norm_ffn2, ffn2_w_in, ffn2_w_out, final_norm):
    n_layer, d = norm_ffn1.shape
    b = x.shape[0]

    mod = _ada_call(c, ada_w, ada_b).reshape(n_layer, b, N_MOD, d)
    rope_tab = _rope_tables(positions)

    f1_in, f1_out = ffn1_w_in.astype(BF16), ffn1_w_out.astype(BF16)
    f2_in, f2_out = ffn2_w_in.astype(BF16), ffn2_w_out.astype(BF16)
    w_main = mix_w_in[:, :, :OFF_IQ].astype(BF16)
    w_idx = jnp.pad(mix_w_in[:, :, OFF_IQ:], ((0, 0), (0, 0), (0, IDX_PAD - (PROJ_WIDTH - OFF_IQ))))
    w_mix_out = mix_w_out.astype(BF16)
    sgu_bT = sgu_b.transpose(0, 2, 1)
    n1 = norm_ffn1.reshape(n_layer, 1, d)
    nm = norm_mix.reshape(n_layer, 1, d)
    n2 = norm_ffn2.reshape(n_layer, 1, d)
    fw = final_norm.reshape(1, d)

    for l in range(n_layer):
        x = _ffn_call(x, mod, l, 0, n1, f1_in, f1_out)
        a_out, q, k, v, qi, ki, wiT = _mix_call(x, mod, l, nm, w_main, w_idx, sgu_w, sgu_bT, rope_tab)
        b_out = _attn_call(q, k, v, qi, ki, wiT)
        x = _ffn_call(x, mod, l, 6, n2, f2_in, f2_out, mix=(a_out, b_out, w_mix_out),
                      final_w=fw if l == n_layer - 1 else None)
    return x
```

```python
import functools

import jax
import jax.numpy as jnp
from jax import lax
from jax.experimental import pallas as pl
from jax.experimental.pallas import tpu as pltpu

F32 = jnp.float32
BF16 = jnp.bfloat16
I32 = jnp.int32

SGU_CHUNK = 128
SGU_GROUPS = 4
SGU_GROUP_DIM = 128
SGU_WIDTH = SGU_GROUPS * SGU_GROUP_DIM
ATT_HEAD_DIM = 64
ATT_HEADS = 8
ATT_WIDTH = ATT_HEADS * ATT_HEAD_DIM
IDX_HEADS = 4
IDX_HEAD_DIM = 64
TOPK_MAX = 256
ROPE_THETA = 500000.0
ROPE_DIM = ATT_HEAD_DIM // 4
ROPE_HALF = ROPE_DIM // 2
RMS_EPS = 1e-6
N_MOD = 9

OFF_SGU_U = 0
OFF_SGU_V = OFF_SGU_U + SGU_WIDTH
OFF_Q = OFF_SGU_V + SGU_WIDTH
OFF_K = OFF_Q + ATT_WIDTH
OFF_V = OFF_K + ATT_WIDTH
OFF_IQ = OFF_V + ATT_WIDTH
OFF_IK = OFF_IQ + IDX_HEADS * IDX_HEAD_DIM
OFF_IW = OFF_IK + IDX_HEAD_DIM
PROJ_WIDTH = OFF_IW + IDX_HEADS

LANES = 128
SUBLANES = 8
FF_CHUNK = 256
TOKEN_BLOCK = 512
ATT_BLOCK = 256
IDX_PAD = 384
COUNT_CHAINS = 4
VMEM_LIMIT = 52 * 1024 * 1024

LOG2_E = 1.4426950408889634
INT_MIN = -2147483648
NEG_BIAS = -1e30


def _resident(block_shape, index_map):
    return pl.BlockSpec(block_shape, index_map, pipeline_mode=pl.Buffered(1))


def _rms_norm(x, w):
    ms = jnp.mean(x * x, axis=-1, keepdims=True)
    return x * lax.rsqrt(ms + RMS_EPS) * w


def _gelu(x):
    return 0.5 * x * (1.0 + lax.erf(x * (0.5 ** 0.5)))


def _split_bf16(x):
    hi = x.astype(BF16)
    lo = (x - hi.astype(F32)).astype(BF16)
    return hi, lo


def _ada_kernel(c_ref, w_ref, b_ref, o_ref):
    ca = jax.nn.silu(c_ref[...]).astype(BF16)
    w = w_ref[...].astype(BF16)
    o_ref[...] = jnp.dot(ca, w, preferred_element_type=F32) + b_ref[...]


def _ada_call(c, ada_w, ada_b):
    n_layer, d, n_out = ada_w.shape
    b = c.shape[0]
    tn = n_out // 4
    return pl.pallas_call(
        _ada_kernel,
        grid=(n_layer, n_out // tn),
        in_specs=[
            pl.BlockSpec((b, d), lambda l, n: (0, 0)),
            pl.BlockSpec((None, d, tn), lambda l, n: (l, 0, n)),
            pl.BlockSpec((None, 1, tn), lambda l, n: (l, 0, n)),
        ],
        out_specs=pl.BlockSpec((None, b, tn), lambda l, n: (l, 0, n)),
        out_shape=jax.ShapeDtypeStruct((n_layer, b, n_out), F32),
        compiler_params=pltpu.CompilerParams(
            dimension_semantics=("arbitrary", "arbitrary"),
            vmem_limit_bytes=VMEM_LIMIT),
        name="ada_mod",
    )(c, ada_w, ada_b.reshape(n_layer, 1, n_out))


def _ffn_kernel(*refs, mod_row, has_mix, has_final):
    it = iter(refs)
    x_ref, mod_ref, nw_ref, win_ref, wout_ref = (next(it) for _ in range(5))
    if has_mix:
        a_ref, b_ref, wmix_ref = (next(it) for _ in range(3))
    if has_final:
        fw_ref = next(it)
    o_ref = next(it)
    acc_ref = next(it)
    d_ff = wout_ref.shape[0]

    x = x_ref[...]
    if has_mix:
        half = a_ref.shape[-1]
        y = jnp.dot(a_ref[...], wmix_ref[0:half, :], preferred_element_type=F32)
        y = y + jnp.dot(b_ref[...], wmix_ref[half:, :], preferred_element_type=F32)
        x = x + mod_ref[5:6, :] * y

    shift = mod_ref[mod_row:mod_row + 1, :]
    scale = mod_ref[mod_row + 1:mod_row + 2, :]
    gate = mod_ref[mod_row + 2:mod_row + 3, :]
    h = _rms_norm(x, nw_ref[...]) * (1.0 + scale) + shift
    hb = h.astype(BF16)

    for ci in range(d_ff // FF_CHUNK):
        cs = slice(ci * FF_CHUNK, (ci + 1) * FF_CHUNK)
        us = slice(d_ff + ci * FF_CHUNK, d_ff + (ci + 1) * FF_CHUNK)
        g = jnp.dot(hb, win_ref[:, cs], preferred_element_type=F32)
        u = jnp.dot(hb, win_ref[:, us], preferred_element_type=F32)
        a = (g * jax.nn.sigmoid(g) * u).astype(BF16)
        part = jnp.dot(a, wout_ref[cs, :], preferred_element_type=F32)
        if ci == 0:
            acc_ref[...] = part
        else:
            acc_ref[...] += part

    out = x + 0.5 * gate * acc_ref[...]
    if has_final:
        out = _rms_norm(out, fw_ref[...])
    o_ref[...] = out


def _ffn_call(x, mod, layer, mod_row, norm_w, w_in, w_out, mix=None, final_w=None):
    b, s, d = x.shape
    tm = min(TOKEN_BLOCK, s)
    d_ff = w_out.shape[1]
    has_mix = mix is not None
    has_final = final_w is not None

    tok = lambda bi, i: (bi, i, 0)
    in_specs = [
        pl.BlockSpec((None, tm, d), tok),
        pl.BlockSpec((None, None, N_MOD, d), lambda bi, i: (layer, bi, 0, 0)),
        pl.BlockSpec((None, 1, d), lambda bi, i: (layer, 0, 0)),
        _resident((None, d, 2 * d_ff), lambda bi, i: (layer, 0, 0)),
        _resident((None, d_ff, d), lambda bi, i: (layer, 0, 0)),
    ]
    args = [x, mod, norm_w, w_in, w_out]
    if has_mix:
        a_out, b_out, w_mix = mix
        in_specs += [
            pl.BlockSpec((None, tm, a_out.shape[-1]), tok),
            pl.BlockSpec((None, tm, b_out.shape[-1]), tok),
            _resident((None, w_mix.shape[1], d), lambda bi, i: (layer, 0, 0)),
        ]
        args += [a_out, b_out, w_mix]
    if has_final:
        in_specs.append(pl.BlockSpec((1, d), lambda bi, i: (0, 0)))
        args.append(final_w)

    return pl.pallas_call(
        functools.partial(_ffn_kernel, mod_row=mod_row, has_mix=has_mix, has_final=has_final),
        grid=(b, s // tm),
        in_specs=in_specs,
        out_specs=pl.BlockSpec((None, tm, d), tok),
        out_shape=jax.ShapeDtypeStruct((b, s, d), F32),
        scratch_shapes=[pltpu.VMEM((tm, d), F32)],
        compiler_params=pltpu.CompilerParams(
            dimension_semantics=("parallel", "parallel"),
            vmem_limit_bytes=VMEM_LIMIT),
        name="ffn_mix" if has_mix else "ffn",
    )(*args)


def _rope(p, tab_c, tab_s1, tab_s2):
    width = p.shape[-1]
    reps = width // LANES
    if reps > 1:
        tab_c = jnp.concatenate([tab_c] * reps, axis=1)
        tab_s1 = jnp.concatenate([tab_s1] * reps, axis=1)
        tab_s2 = jnp.concatenate([tab_s2] * reps, axis=1)
    up = pltpu.roll(p, width - ROPE_HALF, 1)
    down = pltpu.roll(p, ROPE_HALF, 1)
    return p * tab_c + up * tab_s1 + down * tab_s2


def _mix_kernel(x_ref, mod_ref, nw_ref, wm_ref, sw_ref, sb_ref, rope_ref,
                a_ref, q_ref, k_ref, v_ref, qi_ref, ki_ref, wiT_ref):
    tm = x_ref.shape[0]
    x = x_ref[...]
    h = _rms_norm(x, nw_ref[...]) * (1.0 + mod_ref[4:5, :]) + mod_ref[3:4, :]
    hb = h.astype(BF16)

    tab_c = rope_ref[0]
    tab_s1 = rope_ref[1]
    tab_s2 = rope_ref[2]

    u = _gelu(jnp.dot(hb, wm_ref[:, OFF_SGU_U:OFF_SGU_V], preferred_element_type=F32))
    va = _gelu(jnp.dot(hb, wm_ref[:, OFF_SGU_V:OFF_Q], preferred_element_type=F32)).astype(BF16)
    row = lax.broadcasted_iota(I32, (SGU_CHUNK, SGU_CHUNK), 0)
    col = lax.broadcasted_iota(I32, (SGU_CHUNK, SGU_CHUNK), 1)
    causal = col <= row
    for g in range(SGU_GROUPS):
        wg = jnp.where(causal, sw_ref[g], 0.0).astype(BF16)
        bias = sb_ref[:, g:g + 1]
        cs = slice(g * SGU_GROUP_DIM, (g + 1) * SGU_GROUP_DIM)
        for c in range(tm // SGU_CHUNK):
            rs = slice(c * SGU_CHUNK, (c + 1) * SGU_CHUNK)
            mixed = jnp.dot(wg, va[rs, cs], preferred_element_type=F32) + bias
            a_ref[rs, cs] = (u[rs, cs] * mixed).astype(a_ref.dtype)

    q = _rope(jnp.dot(hb, wm_ref[:, OFF_Q:OFF_K], preferred_element_type=F32),
              tab_c, tab_s1, tab_s2) * (ATT_HEAD_DIM ** -0.5 * LOG2_E)
    k = _rope(jnp.dot(hb, wm_ref[:, OFF_K:OFF_V], preferred_element_type=F32),
              tab_c, tab_s1, tab_s2)
    v = jnp.dot(hb, wm_ref[:, OFF_V:OFF_IQ], preferred_element_type=F32)
    ones = jnp.ones((tm, ATT_HEAD_DIM), F32)
    for hd in range(ATT_HEADS):
        hs = slice(hd * ATT_HEAD_DIM, (hd + 1) * ATT_HEAD_DIM)
        q_ref[hd] = q[:, hs].astype(q_ref.dtype)
        k_ref[hd] = k[:, hs].astype(k_ref.dtype)
        v_ref[hd] = jnp.concatenate([v[:, hs], ones], axis=1).astype(v_ref.dtype)

    pi = jnp.dot(hb, wm_ref[:, OFF_IQ:OFF_IQ + IDX_PAD], preferred_element_type=F32)
    n_iq = IDX_HEADS * IDX_HEAD_DIM
    qi = _rope(pi[:, :n_iq], tab_c, tab_s1, tab_s2)
    lane = lax.broadcasted_iota(I32, (tm, LANES), 1)
    is_ki = lane < IDX_HEAD_DIM
    tail = _rope(pi[:, n_iq:], jnp.where(is_ki, tab_c, 1.0),
                 jnp.where(is_ki, tab_s1, 0.0), jnp.where(is_ki, tab_s2, 0.0))
    for hd in range(IDX_HEADS):
        hi, lo = _split_bf16(qi[:, hd * IDX_HEAD_DIM:(hd + 1) * IDX_HEAD_DIM])
        qi_ref[hd] = jnp.concatenate([hi, hi, lo, lo], axis=1)
    hi, lo = _split_bf16(tail[:, :IDX_HEAD_DIM])
    ki_ref[...] = jnp.concatenate([hi, lo, hi, lo], axis=1)
    wiT_ref[...] = tail.T[IDX_HEAD_DIM:IDX_HEAD_DIM + IDX_HEADS, :]


def _mix_call(x, mod, layer, norm_w, w_in, sgu_w, sgu_bT, rope_tab):
    b, s, d = x.shape
    tm = min(TOKEN_BLOCK, s)
    tok = lambda bi, i: (bi, i, 0)
    head_major = lambda bi, i: (bi, 0, i, 0)
    out_shape = [
        jax.ShapeDtypeStruct((b, s, SGU_WIDTH), BF16),
        jax.ShapeDtypeStruct((b, ATT_HEADS, s, ATT_HEAD_DIM), BF16),
        jax.ShapeDtypeStruct((b, ATT_HEADS, s, ATT_HEAD_DIM), BF16),
        jax.ShapeDtypeStruct((b, ATT_HEADS, s, 2 * ATT_HEAD_DIM), BF16),
        jax.ShapeDtypeStruct((b, IDX_HEADS, s, 4 * IDX_HEAD_DIM), BF16),
        jax.ShapeDtypeStruct((b, s, 4 * IDX_HEAD_DIM), BF16),
        jax.ShapeDtypeStruct((b, IDX_HEADS, s), F32),
    ]
    out_specs = [
        pl.BlockSpec((None, tm, SGU_WIDTH), tok),
        pl.BlockSpec((None, ATT_HEADS, tm, ATT_HEAD_DIM), head_major),
        pl.BlockSpec((None, ATT_HEADS, tm, ATT_HEAD_DIM), head_major),
        pl.BlockSpec((None, ATT_HEADS, tm, 2 * ATT_HEAD_DIM), head_major),
        pl.BlockSpec((None, IDX_HEADS, tm, 4 * IDX_HEAD_DIM), head_major),
        pl.BlockSpec((None, tm, 4 * IDX_HEAD_DIM), tok),
        pl.BlockSpec((None, IDX_HEADS, tm), lambda bi, i: (bi, 0, i)),
    ]
    in_specs = [
        pl.BlockSpec((None, tm, d), tok),
        pl.BlockSpec((None, None, N_MOD, d), lambda bi, i: (layer, bi, 0, 0)),
        pl.BlockSpec((None, 1, d), lambda bi, i: (layer, 0, 0)),
        _resident((None, d, OFF_IQ + IDX_PAD), lambda bi, i: (layer, 0, 0)),
        _resident((None, SGU_GROUPS, SGU_CHUNK, SGU_CHUNK), lambda bi, i: (layer, 0, 0, 0)),
        _resident((None, SGU_CHUNK, SGU_GROUPS), lambda bi, i: (layer, 0, 0)),
        pl.BlockSpec((None, 3, tm, LANES), lambda bi, i: (bi, 0, i, 0)),
    ]
    return pl.pallas_call(
        _mix_kernel,
        grid=(b, s // tm),
        in_specs=in_specs,
        out_specs=out_specs,
        out_shape=out_shape,
        compiler_params=pltpu.CompilerParams(
            dimension_semantics=("parallel", "parallel"),
            vmem_limit_bytes=VMEM_LIMIT),
        name="mix_proj",
    )(x, mod, norm_w, w_in, sgu_w, sgu_bT, rope_tab)


def _ordered_bits_to_float(u):
    key = u ^ INT_MIN
    bits = key ^ ((key >> 31) & 0x7FFFFFFF)
    return lax.bitcast_convert_type(bits, F32)


def _floor_bf16(s):
    hb = s.astype(BF16)
    hf = hb.astype(F32)
    bits = lax.bitcast_convert_type(hf, I32)
    step = 1 << 16
    down_bits = jnp.where(hf > 0, bits - step, jnp.where(hf < 0, bits + step, INT_MIN + step))
    down = lax.bitcast_convert_type(down_bits, F32).astype(BF16)
    return jnp.where(hf > s, down, hb)


def _attn_kernel(q_ref, k_ref, v_ref, qi_ref, ki_ref, wiT_ref, o_ref,
                 sc_ref, sb_ref, bias_ref, m_ref, acc_ref, *, topk, seq_bits):
    tq = q_ref.shape[1]
    kc = tq
    n_head, hd = q_ref.shape[0], q_ref.shape[2]
    j = pl.program_id(1)
    n_tiles = j + 1
    q_start = j * tq
    nt_dims = (((1,), (1,)), ((), ()))

    qi_all = qi_ref[...].reshape(IDX_HEADS * tq, qi_ref.shape[-1])
    wi = wiT_ref[...]
    q_idx = q_start + lax.broadcasted_iota(I32, (kc, tq), 1)
    k_iota = lax.broadcasted_iota(I32, (kc, tq), 0)
    full = q_start + lax.broadcasted_iota(I32, (1, tq), 1) + 1 >= topk

    def select_keys(n):
        for kt in range(n):
            logits = lax.dot_general(ki_ref[kt * kc:(kt + 1) * kc, :], qi_all, nt_dims,
                                     preferred_element_type=F32)
            sc = wi[0:1, :] * jnp.maximum(logits[:, 0:tq], 0.0)
            for h in range(1, IDX_HEADS):
                sc = sc + wi[h:h + 1, :] * jnp.maximum(logits[:, h * tq:(h + 1) * tq], 0.0)
            sc = sc + 0.0
            if kt == n - 1:
                sc = jnp.where(kt * kc + k_iota <= q_idx, sc, -jnp.inf)
            sc_ref[kt] = sc
            sb_ref[kt] = _floor_bf16(sc)

        def count(pred):
            acc = jnp.zeros((COUNT_CHAINS, SUBLANES, tq), I32)
            for kt in range(n):
                ind = jnp.where(pred(sc_ref[kt], kt * kc + k_iota), 1, 0)
                acc = acc + jnp.sum(ind.reshape(COUNT_CHAINS, -1, SUBLANES, tq), axis=1)
            return jnp.sum(acc.reshape(-1, tq), axis=0, keepdims=True)

        def bisect(i, t):
            cand = t | jnp.left_shift(jnp.int32(1), 31 - i)
            cand_f = _ordered_bits_to_float(cand)
            cnt = count(lambda s, _: s >= cand_f)
            return jnp.where(cnt >= topk, cand, t)

        def count_bf16(cand_b):
            acc = jnp.zeros((COUNT_CHAINS, 2 * SUBLANES, tq), BF16)
            one, zero = jnp.ones((), BF16), jnp.zeros((), BF16)
            for kt in range(n):
                ind = jnp.where(sb_ref[kt] >= cand_b, one, zero)
                ind = ind.reshape(COUNT_CHAINS, -1, 2 * SUBLANES, tq)
                for g in range(ind.shape[1]):
                    acc = acc + ind[:, g]
            return jnp.sum(acc.astype(F32).reshape(-1, tq), axis=0, keepdims=True)

        def bisect_bf16(i, t):
            cand = t | jnp.left_shift(jnp.int32(1), 31 - i)
            cnt = count_bf16(_ordered_bits_to_float(cand).astype(BF16))
            return jnp.where(cnt >= topk, cand, t)

        t_hi = lax.fori_loop(0, 16, bisect_bf16, jnp.zeros((1, tq), I32))
        t_bits = lax.fori_loop(16, 32, bisect, t_hi)
        thr = jnp.where(full, _ordered_bits_to_float(t_bits), -jnp.inf)

        n_gt = count(lambda s, _: s > thr)
        n_eq = count(lambda s, _: s == thr)
        need = topk - n_gt
        tie_rows = jnp.where(full & (n_eq > need), 1, 0)

        def resolve_ties():
            def step(i, p):
                cand = p | jnp.left_shift(jnp.int32(1), seq_bits - 1 - i)
                cnt = count(lambda s, ki: (s == thr) & (ki < cand))
                return jnp.where(cnt < need, cand, p)
            return lax.fori_loop(0, seq_bits, step, jnp.zeros((1, tq), I32))

        last_tie = lax.cond(jnp.max(tie_rows) > 0, resolve_ties,
                            lambda: jnp.full((1, tq), (1 << seq_bits) - 1, I32))
        last_tie = jnp.where(full, last_tie, -1)

        for kt in range(n):
            s = sc_ref[kt]
            sel = (s > thr) | ((s == thr) & (kt * kc + k_iota <= last_tie))
            bias_ref[kt] = jnp.where(sel, 0.0, NEG_BIAS).T
        return 0

    lax.switch(j, [functools.partial(select_keys, n + 1) for n in range(bias_ref.shape[0])])

    m_ref[...] = jnp.full_like(m_ref, -jnp.inf)
    acc_ref[...] = jnp.zeros_like(acc_ref)

    def attend_tile(kt):
        k0 = pl.multiple_of(kt * kc, kc)
        bias = bias_ref[kt]
        for h in range(n_head):
            s = lax.dot_general(q_ref[h], k_ref[h, pl.ds(k0, kc), :], nt_dims,
                                preferred_element_type=F32) + bias
            m_prev = m_ref[h]
            m_new = jnp.maximum(m_prev, jnp.max(s, axis=1, keepdims=True))
            alpha = jnp.exp2(m_prev - m_new)
            p = jnp.exp2(s - jnp.concatenate([m_new] * (kc // LANES), axis=1))
            pv = jnp.dot(p.astype(BF16), v_ref[h, pl.ds(k0, kc), :],
                         preferred_element_type=F32)
            acc_ref[h] = acc_ref[h] * alpha + pv
            m_ref[h] = m_new

    odd = n_tiles % 2

    @pl.when(odd == 1)
    def _():
        attend_tile(0)

    def attend_pair(kp, carry):
        attend_tile(odd + 2 * kp)
        attend_tile(odd + 2 * kp + 1)
        return carry

    lax.fori_loop(0, n_tiles // 2, attend_pair, 0)

    outs = []
    for h in range(n_head):
        acc = acc_ref[h]
        outs.append(acc[:, :hd] / acc[:, hd:])
    o_ref[...] = jnp.concatenate(outs, axis=1).astype(o_ref.dtype)


def _attn_call(q, k, v, qi, ki, wiT):
    b, n_head, s, hd = q.shape
    tq = min(ATT_BLOCK, s)
    n_blk = s // tq
    topk = min(TOPK_MAX, s // 4)
    seq_bits = max(1, (s - 1).bit_length())
    blk = lambda bi, j: (bi, 0, j, 0)
    whole = lambda bi, j: (bi, 0, 0, 0)
    return pl.pallas_call(
        functools.partial(_attn_kernel, topk=topk, seq_bits=seq_bits),
        grid=(b, n_blk),
        in_specs=[
            pl.BlockSpec((None, n_head, tq, hd), blk),
            pl.BlockSpec((None, n_head, s, hd), whole),
            pl.BlockSpec((None, n_head, s, v.shape[-1]), whole),
            pl.BlockSpec((None, IDX_HEADS, tq, qi.shape[-1]), blk),
            pl.BlockSpec((None, s, ki.shape[-1]), lambda bi, j: (bi, 0, 0)),
            pl.BlockSpec((None, IDX_HEADS, tq), lambda bi, j: (bi, 0, j)),
        ],
        out_specs=pl.BlockSpec((None, tq, n_head * hd), lambda bi, j: (bi, j, 0)),
        out_shape=jax.ShapeDtypeStruct((b, s, n_head * hd), BF16),
        scratch_shapes=[
            pltpu.VMEM((n_blk, tq, tq), F32),
            pltpu.VMEM((n_blk, tq, tq), BF16),
            pltpu.VMEM((n_blk, tq, tq), F32),
            pltpu.VMEM((n_head, tq, LANES), F32),
            pltpu.VMEM((n_head, tq, v.shape[-1]), F32),
        ],
        compiler_params=pltpu.CompilerParams(
            dimension_semantics=("parallel", "arbitrary"),
            vmem_limit_bytes=VMEM_LIMIT),
        name="dsa_attention",
    )(q, k, v, qi, ki, wiT)


def _rope_tables(positions):
    inv_freq = ROPE_THETA ** (-jnp.arange(0, ROPE_DIM, 2, dtype=F32) / ROPE_DIM)
    ang = positions.astype(F32)[..., None] * inv_freq
    cos, sin = lax.optimization_barrier((jnp.cos(ang), jnp.sin(ang)))
    rest = ATT_HEAD_DIM - ROPE_DIM
    ones = jnp.ones(cos.shape[:-1] + (rest,), F32)
    zeros = jnp.zeros(cos.shape[:-1] + (rest,), F32)
    zhalf = jnp.zeros_like(sin)
    tab_c = jnp.concatenate([cos, cos, ones], axis=-1)
    tab_s1 = jnp.concatenate([-sin, zhalf, zeros], axis=-1)
    tab_s2 = jnp.concatenate([zhalf, sin, zeros], axis=-1)
    tabs = jnp.stack([tab_c, tab_s1, tab_s2], axis=1)
    return jnp.concatenate([tabs] * (LANES // ATT_HEAD_DIM), axis=-1)


def kernel(x, c, positions, ada_w, ada_b, norm_ffn1, ffn1_w_in, ffn1_w_out, norm_mix, mix_w_in,
           sgu_w, sgu_b, mix_w_out, norm_ffn2, ffn2_w_in, ffn2_w_out, final_norm):
    n_layer, d = norm_ffn1.shape
    b = x.shape[0]

    mod = _ada_call(c, ada_w, ada_b).reshape(n_layer, b, N_MOD, d)
    rope_tab = _rope_tables(positions)

    f1_in, f1_out = ffn1_w_in.astype(BF16), ffn1_w_out.astype(BF16)
    f2_in, f2_out = ffn2_w_in.astype(BF16), ffn2_w_out.astype(BF16)
    w_mix_in = jnp.pad(mix_w_in.astype(BF16), ((0, 0), (0, 0), (0, OFF_IQ + IDX_PAD - PROJ_WIDTH)))
    w_mix_out = mix_w_out.astype(BF16)
    sgu_bT = sgu_b.transpose(0, 2, 1)
    n1 = norm_ffn1.reshape(n_layer, 1, d)
    nm = norm_mix.reshape(n_layer, 1, d)
    n2 = norm_ffn2.reshape(n_layer, 1, d)
    fw = final_norm.reshape(1, d)

    for l in range(n_layer):
        x = _ffn_call(x, mod, l, 0, n1, f1_in, f1_out)
        a_out, q, k, v, qi, ki, wiT = _mix_call(x, mod, l, nm, w_mix_in, sgu_w, sgu_bT, rope_tab)
        b_out = _attn_call(q, k, v, qi, ki, wiT)
        x = _ffn_call(x, mod, l, 6, n2, f2_in, f2_out, mix=(a_out, b_out, w_mix_out),
                      final_w=fw if l == n_layer - 1 else None)
    return x
```

```python
import functools

import jax
import jax.numpy as jnp
from jax import lax
from jax.experimental import pallas as pl
from jax.experimental.pallas import tpu as pltpu

F32 = jnp.float32
BF16 = jnp.bfloat16
I32 = jnp.int32

SGU_CHUNK = 128
SGU_GROUPS = 4
SGU_GROUP_DIM = 128
SGU_WIDTH = SGU_GROUPS * SGU_GROUP_DIM
ATT_HEAD_DIM = 64
ATT_HEADS = 8
ATT_WIDTH = ATT_HEADS * ATT_HEAD_DIM
IDX_HEADS = 4
IDX_HEAD_DIM = 64
TOPK_MAX = 256
ROPE_THETA = 500000.0
ROPE_DIM = ATT_HEAD_DIM // 4
ROPE_HALF = ROPE_DIM // 2
RMS_EPS = 1e-6
N_MOD = 9

OFF_SGU_U = 0
OFF_SGU_V = OFF_SGU_U + SGU_WIDTH
OFF_Q = OFF_SGU_V + SGU_WIDTH
OFF_K = OFF_Q + ATT_WIDTH
OFF_V = OFF_K + ATT_WIDTH
OFF_IQ = OFF_V + ATT_WIDTH
OFF_IK = OFF_IQ + IDX_HEADS * IDX_HEAD_DIM
OFF_IW = OFF_IK + IDX_HEAD_DIM
PROJ_WIDTH = OFF_IW + IDX_HEADS

LANES = 128
SUBLANES = 8
FF_CHUNK = 256
TOKEN_BLOCK = 512
ATT_BLOCK = 256
IDX_PAD = 384
COUNT_CHAINS = 4
VMEM_LIMIT = 56 * 1024 * 1024

LOG2_E = 1.4426950408889634
INT_MIN = -2147483648
NEG_BIAS = -1e30


def _resident(block_shape, index_map):
    return pl.BlockSpec(block_shape, index_map, pipeline_mode=pl.Buffered(1))


def _rms_norm(x, w):
    ms = jnp.mean(x * x, axis=-1, keepdims=True)
    return x * lax.rsqrt(ms + RMS_EPS) * w


def _gelu(x):
    return 0.5 * x * (1.0 + lax.erf(x * (0.5 ** 0.5)))


def _split_bf16(x):
    hi = x.astype(BF16)
    lo = (x - hi.astype(F32)).astype(BF16)
    return hi, lo


def _ada_kernel(c_ref, w_ref, b_ref, o_ref):
    ca = jax.nn.silu(c_ref[...]).astype(BF16)
    w = w_ref[...].astype(BF16)
    o_ref[...] = jnp.dot(ca, w, preferred_element_type=F32) + b_ref[...]


def _ada_call(c, ada_w, ada_b):
    n_layer, d, n_out = ada_w.shape
    b = c.shape[0]
    tn = n_out // 4
    return pl.pallas_call(
        _ada_kernel,
        grid=(n_layer, n_out // tn),
        in_specs=[
            pl.BlockSpec((b, d), lambda l, n: (0, 0)),
            pl.BlockSpec((None, d, tn), lambda l, n: (l, 0, n)),
            pl.BlockSpec((None, 1, tn), lambda l, n: (l, 0, n)),
        ],
        out_specs=pl.BlockSpec((None, b, tn), lambda l, n: (l, 0, n)),
        out_shape=jax.ShapeDtypeStruct((n_layer, b, n_out), F32),
        compiler_params=pltpu.CompilerParams(
            dimension_semantics=("arbitrary", "arbitrary"),
            vmem_limit_bytes=VMEM_LIMIT),
        name="ada_mod",
    )(c, ada_w, ada_b.reshape(n_layer, 1, n_out))


def _ffn_kernel(*refs, mod_row, has_mix, has_final):
    it = iter(refs)
    x_ref, mod_ref, nw_ref, win_ref, wout_ref = (next(it) for _ in range(5))
    if has_mix:
        a_ref, b_ref, wmix_ref = (next(it) for _ in range(3))
    if has_final:
        fw_ref = next(it)
    o_ref = next(it)
    acc_ref = next(it)
    d_ff = wout_ref.shape[0]

    x = x_ref[...]
    if has_mix:
        half = a_ref.shape[-1]
        y = jnp.dot(a_ref[...], wmix_ref[0:half, :].astype(BF16), preferred_element_type=F32)
        y = y + jnp.dot(b_ref[...], wmix_ref[half:, :].astype(BF16), preferred_element_type=F32)
        x = x + mod_ref[5:6, :] * y

    shift = mod_ref[mod_row:mod_row + 1, :]
    scale = mod_ref[mod_row + 1:mod_row + 2, :]
    gate = mod_ref[mod_row + 2:mod_row + 3, :]
    h = _rms_norm(x, nw_ref[...]) * (1.0 + scale) + shift
    hb = h.astype(BF16)

    for ci in range(d_ff // FF_CHUNK):
        cs = slice(ci * FF_CHUNK, (ci + 1) * FF_CHUNK)
        us = slice(d_ff + ci * FF_CHUNK, d_ff + (ci + 1) * FF_CHUNK)
        g = jnp.dot(hb, win_ref[:, cs].astype(BF16), preferred_element_type=F32)
        u = jnp.dot(hb, win_ref[:, us].astype(BF16), preferred_element_type=F32)
        a = (g * jax.nn.sigmoid(g) * u).astype(BF16)
        part = jnp.dot(a, wout_ref[cs, :].astype(BF16), preferred_element_type=F32)
        if ci == 0:
            acc_ref[...] = part
        else:
            acc_ref[...] += part

    out = x + 0.5 * gate * acc_ref[...]
    if has_final:
        out = _rms_norm(out, fw_ref[...])
    o_ref[...] = out


def _ffn_call(x, mod, layer, mod_row, norm_w, w_in, w_out, mix=None, final_w=None):
    b, s, d = x.shape
    tm = min(TOKEN_BLOCK, s)
    d_ff = w_out.shape[1]
    has_mix = mix is not None
    has_final = final_w is not None

    tok = lambda bi, i: (bi, i, 0)
    in_specs = [
        pl.BlockSpec((None, tm, d), tok),
        pl.BlockSpec((None, None, N_MOD, d), lambda bi, i: (layer, bi, 0, 0)),
        pl.BlockSpec((None, 1, d), lambda bi, i: (layer, 0, 0)),
        _resident((None, d, 2 * d_ff), lambda bi, i: (layer, 0, 0)),
        _resident((None, d_ff, d), lambda bi, i: (layer, 0, 0)),
    ]
    args = [x, mod, norm_w, w_in, w_out]
    if has_mix:
        a_out, b_out, w_mix = mix
        in_specs += [
            pl.BlockSpec((None, tm, a_out.shape[-1]), tok),
            pl.BlockSpec((None, tm, b_out.shape[-1]), tok),
            _resident((None, w_mix.shape[1], d), lambda bi, i: (layer, 0, 0)),
        ]
        args += [a_out, b_out, w_mix]
    if has_final:
        in_specs.append(pl.BlockSpec((1, d), lambda bi, i: (0, 0)))
        args.append(final_w)

    return pl.pallas_call(
        functools.partial(_ffn_kernel, mod_row=mod_row, has_mix=has_mix, has_final=has_final),
        grid=(b, s // tm),
        in_specs=in_specs,
        out_specs=pl.BlockSpec((None, tm, d), tok),
        out_shape=jax.ShapeDtypeStruct((b, s, d), F32),
        scratch_shapes=[pltpu.VMEM((tm, d), F32)],
        compiler_params=pltpu.CompilerParams(
            dimension_semantics=("parallel", "parallel"),
            vmem_limit_bytes=VMEM_LIMIT),
        name="ffn_mix" if has_mix else "ffn",
    )(*args)


def _rope(p, tab_c, tab_s1, tab_s2):
    width = p.shape[-1]
    reps = width // LANES
    if reps > 1:
        tab_c = jnp.concatenate([tab_c] * reps, axis=1)
        tab_s1 = jnp.concatenate([tab_s1] * reps, axis=1)
        tab_s2 = jnp.concatenate([tab_s2] * reps, axis=1)
    up = pltpu.roll(p, width - ROPE_HALF, 1)
    down = pltpu.roll(p, ROPE_HALF, 1)
    return p * tab_c + up * tab_s1 + down * tab_s2


def _mix_kernel(x_ref, mod_ref, nw_ref, wm_ref, sw_ref, sb_ref, rope_ref,
                a_ref, q_ref, k_ref, v_ref, qi_ref, ki_ref, wiT_ref):
    tm = x_ref.shape[0]
    x = x_ref[...]
    h = _rms_norm(x, nw_ref[...]) * (1.0 + mod_ref[4:5, :]) + mod_ref[3:4, :]
    hb = h.astype(BF16)

    tab_c = rope_ref[0]
    tab_s1 = rope_ref[1]
    tab_s2 = rope_ref[2]

    def proj(lo, hi):
        return jnp.dot(hb, wm_ref[:, lo:hi].astype(BF16), preferred_element_type=F32)

    u = _gelu(proj(OFF_SGU_U, OFF_SGU_V))
    va = _gelu(proj(OFF_SGU_V, OFF_Q)).astype(BF16)
    row = lax.broadcasted_iota(I32, (SGU_CHUNK, SGU_CHUNK), 0)
    col = lax.broadcasted_iota(I32, (SGU_CHUNK, SGU_CHUNK), 1)
    causal = col <= row
    for g in range(SGU_GROUPS):
        wg = jnp.where(causal, sw_ref[g], 0.0).astype(BF16)
        bias = sb_ref[:, g:g + 1]
        cs = slice(g * SGU_GROUP_DIM, (g + 1) * SGU_GROUP_DIM)
        for c in range(tm // SGU_CHUNK):
            rs = slice(c * SGU_CHUNK, (c + 1) * SGU_CHUNK)
            mixed = jnp.dot(wg, va[rs, cs], preferred_element_type=F32) + bias
            a_ref[rs, cs] = (u[rs, cs] * mixed).astype(a_ref.dtype)

    q = _rope(proj(OFF_Q, OFF_K), tab_c, tab_s1, tab_s2) * (ATT_HEAD_DIM ** -0.5 * LOG2_E)
    k = _rope(proj(OFF_K, OFF_V), tab_c, tab_s1, tab_s2)
    v = proj(OFF_V, OFF_IQ)
    ones = jnp.ones((tm, ATT_HEAD_DIM), F32)
    for hd in range(ATT_HEADS):
        hs = slice(hd * ATT_HEAD_DIM, (hd + 1) * ATT_HEAD_DIM)
        q_ref[hd] = q[:, hs].astype(q_ref.dtype)
        k_ref[hd] = k[:, hs].astype(k_ref.dtype)
        v_ref[hd] = jnp.concatenate([v[:, hs], ones], axis=1).astype(v_ref.dtype)

    pi = jnp.concatenate([proj(OFF_IQ, PROJ_WIDTH),
                          jnp.zeros((tm, OFF_IQ + IDX_PAD - PROJ_WIDTH), F32)], axis=1)
    n_iq = IDX_HEADS * IDX_HEAD_DIM
    qi = _rope(pi[:, :n_iq], tab_c, tab_s1, tab_s2)
    lane = lax.broadcasted_iota(I32, (tm, LANES), 1)
    is_ki = lane < IDX_HEAD_DIM
    tail = _rope(pi[:, n_iq:], jnp.where(is_ki, tab_c, 1.0),
                 jnp.where(is_ki, tab_s1, 0.0), jnp.where(is_ki, tab_s2, 0.0))
    for hd in range(IDX_HEADS):
        hi, lo = _split_bf16(qi[:, hd * IDX_HEAD_DIM:(hd + 1) * IDX_HEAD_DIM])
        qi_ref[hd] = jnp.concatenate([hi, hi, lo, lo], axis=1)
    hi, lo = _split_bf16(tail[:, :IDX_HEAD_DIM])
    ki_ref[...] = jnp.concatenate([hi, lo, hi, lo], axis=1)
    wiT_ref[...] = tail.T[IDX_HEAD_DIM:IDX_HEAD_DIM + IDX_HEADS, :]


def _mix_call(x, mod, layer, norm_w, w_in, sgu_w, sgu_bT, rope_tab):
    b, s, d = x.shape
    tm = min(TOKEN_BLOCK, s)
    tok = lambda bi, i: (bi, i, 0)
    head_major = lambda bi, i: (bi, 0, i, 0)
    out_shape = [
        jax.ShapeDtypeStruct((b, s, SGU_WIDTH), BF16),
        jax.ShapeDtypeStruct((b, ATT_HEADS, s, ATT_HEAD_DIM), BF16),
        jax.ShapeDtypeStruct((b, ATT_HEADS, s, ATT_HEAD_DIM), BF16),
        jax.ShapeDtypeStruct((b, ATT_HEADS, s, 2 * ATT_HEAD_DIM), BF16),
        jax.ShapeDtypeStruct((b, IDX_HEADS, s, 4 * IDX_HEAD_DIM), BF16),
        jax.ShapeDtypeStruct((b, s, 4 * IDX_HEAD_DIM), BF16),
        jax.ShapeDtypeStruct((b, IDX_HEADS, s), F32),
    ]
    out_specs = [
        pl.BlockSpec((None, tm, SGU_WIDTH), tok),
        pl.BlockSpec((None, ATT_HEADS, tm, ATT_HEAD_DIM), head_major),
        pl.BlockSpec((None, ATT_HEADS, tm, ATT_HEAD_DIM), head_major),
        pl.BlockSpec((None, ATT_HEADS, tm, 2 * ATT_HEAD_DIM), head_major),
        pl.BlockSpec((None, IDX_HEADS, tm, 4 * IDX_HEAD_DIM), head_major),
        pl.BlockSpec((None, tm, 4 * IDX_HEAD_DIM), tok),
        pl.BlockSpec((None, IDX_HEADS, tm), lambda bi, i: (bi, 0, i)),
    ]
    in_specs = [
        pl.BlockSpec((None, tm, d), tok),
        pl.BlockSpec((None, None, N_MOD, d), lambda bi, i: (layer, bi, 0, 0)),
        pl.BlockSpec((None, 1, d), lambda bi, i: (layer, 0, 0)),
        _resident((None, d, PROJ_WIDTH), lambda bi, i: (layer, 0, 0)),
        _resident((None, SGU_GROUPS, SGU_CHUNK, SGU_CHUNK), lambda bi, i: (layer, 0, 0, 0)),
        _resident((None, SGU_CHUNK, SGU_GROUPS), lambda bi, i: (layer, 0, 0)),
        pl.BlockSpec((None, 3, tm, LANES), lambda bi, i: (bi, 0, i, 0)),
    ]
    return pl.pallas_call(
        _mix_kernel,
        grid=(b, s // tm),
        in_specs=in_specs,
        out_specs=out_specs,
        out_shape=out_shape,
        compiler_params=pltpu.CompilerParams(
            dimension_semantics=("parallel", "parallel"),
            vmem_limit_bytes=VMEM_LIMIT),
        name="mix_proj",
    )(x, mod, norm_w, w_in, sgu_w, sgu_bT, rope_tab)


def _ordered_bits_to_float(u):
    key = u ^ INT_MIN
    bits = key ^ ((key >> 31) & 0x7FFFFFFF)
    return lax.bitcast_convert_type(bits, F32)


def _floor_bf16(s):
    hb = s.astype(BF16)
    hf = hb.astype(F32)
    bits = lax.bitcast_convert_type(hf, I32)
    step = 1 << 16
    down_bits = jnp.where(hf > 0, bits - step, jnp.where(hf < 0, bits + step, INT_MIN + step))
    down = lax.bitcast_convert_type(down_bits, F32).astype(BF16)
    return jnp.where(hf > s, down, hb)


def _attn_kernel(q_ref, k_ref, v_ref, qi_ref, ki_ref, wiT_ref, o_ref,
                 sc_ref, sb_ref, bias_ref, m_ref, acc_ref, *, topk, seq_bits):
    tq = q_ref.shape[1]
    kc = tq
    n_head, hd = q_ref.shape[0], q_ref.shape[2]
    j = pl.program_id(1)
    n_tiles = j + 1
    q_start = j * tq
    nt_dims = (((1,), (1,)), ((), ()))

    qi_all = qi_ref[...].reshape(IDX_HEADS * tq, qi_ref.shape[-1])
    wi = wiT_ref[...]
    q_idx = q_start + lax.broadcasted_iota(I32, (kc, tq), 1)
    k_iota = lax.broadcasted_iota(I32, (kc, tq), 0)
    full = q_start + lax.broadcasted_iota(I32, (1, tq), 1) + 1 >= topk

    def select_keys(n):
        for kt in range(n):
            logits = lax.dot_general(ki_ref[kt * kc:(kt + 1) * kc, :], qi_all, nt_dims,
                                     preferred_element_type=F32)
            sc = wi[0:1, :] * jnp.maximum(logits[:, 0:tq], 0.0)
            for h in range(1, IDX_HEADS):
                sc = sc + wi[h:h + 1, :] * jnp.maximum(logits[:, h * tq:(h + 1) * tq], 0.0)
            sc = sc + 0.0
            if kt == n - 1:
                sc = jnp.where(kt * kc + k_iota <= q_idx, sc, -jnp.inf)
            sc_ref[kt] = sc
            sb_ref[kt] = _floor_bf16(sc)

        def count(pred):
            acc = jnp.zeros((COUNT_CHAINS, SUBLANES, tq), I32)
            for kt in range(n):
                ind = jnp.where(pred(sc_ref[kt], kt * kc + k_iota), 1, 0)
                acc = acc + jnp.sum(ind.reshape(COUNT_CHAINS, -1, SUBLANES, tq), axis=1)
            return jnp.sum(acc.reshape(-1, tq), axis=0, keepdims=True)

        def bisect(i, t):
            cand = t | jnp.left_shift(jnp.int32(1), 31 - i)
            cand_f = _ordered_bits_to_float(cand)
            cnt = count(lambda s, _: s >= cand_f)
            return jnp.where(cnt >= topk, cand, t)

        def count_bf16(cand_b):
            acc = jnp.zeros((COUNT_CHAINS, 2 * SUBLANES, tq), BF16)
            one, zero = jnp.ones((), BF16), jnp.zeros((), BF16)
            for kt in range(n):
                ind = jnp.where(sb_ref[kt] >= cand_b, one, zero)
                ind = ind.reshape(COUNT_CHAINS, -1, 2 * SUBLANES, tq)
                for g in range(ind.shape[1]):
                    acc = acc + ind[:, g]
            return jnp.sum(acc.astype(F32).reshape(-1, tq), axis=0, keepdims=True)

        def bisect_bf16(i, t):
            cand = t | jnp.left_shift(jnp.int32(1), 31 - i)
            cnt = count_bf16(_ordered_bits_to_float(cand).astype(BF16))
            return jnp.where(cnt >= topk, cand, t)

        t_hi = lax.fori_loop(0, 16, bisect_bf16, jnp.zeros((1, tq), I32))
        t_bits = lax.fori_loop(16, 32, bisect, t_hi)
        thr = jnp.where(full, _ordered_bits_to_float(t_bits), -jnp.inf)

        n_gt = count(lambda s, _: s > thr)
        n_eq = count(lambda s, _: s == thr)
        need = topk - n_gt
        tie_rows = jnp.where(full & (n_eq > need), 1, 0)

        def resolve_ties():
            def step(i, p):
                cand = p | jnp.left_shift(jnp.int32(1), seq_bits - 1 - i)
                cnt = count(lambda s, ki: (s == thr) & (ki < cand))
                return jnp.where(cnt < need, cand, p)
            return lax.fori_loop(0, seq_bits, step, jnp.zeros((1, tq), I32))

        last_tie = lax.cond(jnp.max(tie_rows) > 0, resolve_ties,
                            lambda: jnp.full((1, tq), (1 << seq_bits) - 1, I32))
        last_tie = jnp.where(full, last_tie, -1)

        for kt in range(n):
            s = sc_ref[kt]
            sel = (s > thr) | ((s == thr) & (kt * kc + k_iota <= last_tie))
            bias_ref[kt] = jnp.where(sel, 0.0, NEG_BIAS).T
        return 0

    lax.switch(j, [functools.partial(select_keys, n + 1) for n in range(bias_ref.shape[0])])

    m_ref[...] = jnp.full_like(m_ref, -jnp.inf)
    acc_ref[...] = jnp.zeros_like(acc_ref)

    def attend_tile(kt):
        k0 = pl.multiple_of(kt * kc, kc)
        bias = bias_ref[kt]
        for h in range(n_head):
            s = lax.dot_general(q_ref[h], k_ref[h, pl.ds(k0, kc), :], nt_dims,
                                preferred_element_type=F32) + bias
            m_prev = m_ref[h]
            m_new = jnp.maximum(m_prev, jnp.max(s, axis=1, keepdims=True))
            alpha = jnp.exp2(m_prev - m_new)
            p = jnp.exp2(s - jnp.concatenate([m_new] * (kc // LANES), axis=1))
            pv = jnp.dot(p.astype(BF16), v_ref[h, pl.ds(k0, kc), :],
                         preferred_element_type=F32)
            acc_ref[h] = acc_ref[h] * alpha + pv
            m_ref[h] = m_new

    single = n_tiles % 2
    pair = (n_tiles // 2) % 2

    @pl.when(single == 1)
    def _():
        attend_tile(0)

    @pl.when(pair == 1)
    def _():
        attend_tile(single)
        attend_tile(single + 1)

    def attend_quad(kq, carry):
        for t in range(4):
            attend_tile(single + 2 * pair + 4 * kq + t)
        return carry

    lax.fori_loop(0, n_tiles // 4, attend_quad, 0)

    outs = []
    for h in range(n_head):
        acc = acc_ref[h]
        outs.append(acc[:, :hd] / acc[:, hd:])
    o_ref[...] = jnp.concatenate(outs, axis=1).astype(o_ref.dtype)


def _attn_call(q, k, v, qi, ki, wiT):
    b, n_head, s, hd = q.shape
    tq = min(ATT_BLOCK, s)
    n_blk = s // tq
    topk = min(TOPK_MAX, s // 4)
    seq_bits = max(1, (s - 1).bit_length())
    blk = lambda bi, j: (bi, 0, j, 0)
    whole = lambda bi, j: (bi, 0, 0, 0)
    return pl.pallas_call(
        functools.partial(_attn_kernel, topk=topk, seq_bits=seq_bits),
        grid=(b, n_blk),
        in_specs=[
            pl.BlockSpec((None, n_head, tq, hd), blk),
            pl.BlockSpec((None, n_head, s, hd), whole),
            pl.BlockSpec((None, n_head, s, v.shape[-1]), whole),
            pl.BlockSpec((None, IDX_HEADS, tq, qi.shape[-1]), blk),
            pl.BlockSpec((None, s, ki.shape[-1]), lambda bi, j: (bi, 0, 0)),
            pl.BlockSpec((None, IDX_HEADS, tq), lambda bi, j: (bi, 0, j)),
        ],
        out_specs=pl.BlockSpec((None, tq, n_head * hd), lambda bi, j: (bi, j, 0)),
        out_shape=jax.ShapeDtypeStruct((b, s, n_head * hd), BF16),
        scratch_shapes=[
            pltpu.VMEM((n_blk, tq, tq), F32),
            pltpu.VMEM((n_blk, tq, tq), BF16),
            pltpu.VMEM((n_blk, tq, tq), F32),
            pltpu.VMEM((n_head, tq, LANES), F32),
            pltpu.VMEM((n_head, tq, v.shape[-1]), F32),
        ],
        compiler_params=pltpu.CompilerParams(
            dimension_semantics=("parallel", "arbitrary"),
            vmem_limit_bytes=VMEM_LIMIT),
        name="dsa_attention",
    )(q, k, v, qi, ki, wiT)


def _rope_tables(positions):
    inv_freq = ROPE_THETA ** (-jnp.arange(0, ROPE_DIM, 2, dtype=F32) / ROPE_DIM)
    ang = positions.astype(F32)[..., None] * inv_freq
    flat = lax.optimization_barrier(ang.reshape(ang.shape[0], -1))
    cos, sin = lax.optimization_barrier((jnp.cos(flat), jnp.sin(flat)))
    cos, sin = cos.reshape(ang.shape), sin.reshape(ang.shape)
    rest = ATT_HEAD_DIM - ROPE_DIM
    ones = jnp.ones(cos.shape[:-1] + (rest,), F32)
    zeros = jnp.zeros(cos.shape[:-1] + (rest,), F32)
    zhalf = jnp.zeros_like(sin)
    tab_c = jnp.concatenate([cos, cos, ones], axis=-1)
    tab_s1 = jnp.concatenate([-sin, zhalf, zeros], axis=-1)
    tab_s2 = jnp.concatenate([zhalf, sin, zeros], axis=-1)
    tabs = jnp.stack([tab_c, tab_s1, tab_s2], axis=1)
    return jnp.concatenate([tabs] * (LANES // ATT_HEAD_DIM), axis=-1)


def kernel(x, c, positions, ada_w, ada_b, norm_ffn1, ffn1_w_in, ffn1_w_out, norm_mix, mix_w_in,
           sgu_w, sgu_b, mix_w_out, norm_ffn2, ffn2_w_in, ffn2_w_out, final_norm):
    n_layer, d = norm_ffn1.shape
    b = x.shape[0]

    mod = _ada_call(c, ada_w, ada_b).reshape(n_layer, b, N_MOD, d)
    rope_tab = _rope_tables(positions)

    f1_in, f1_out = ffn1_w_in, ffn1_w_out
    f2_in, f2_out = ffn2_w_in, ffn2_w_out
    w_mix_in, w_mix_out = mix_w_in, mix_w_out
    sgu_bT = sgu_b.transpose(0, 2, 1)
    n1 = norm_ffn1.reshape(n_layer, 1, d)
    nm = norm_mix.reshape(n_layer, 1, d)
    n2 = norm_ffn2.reshape(n_layer, 1, d)
    fw = final_norm.reshape(1, d)

    for l in range(n_layer):
        x = _ffn_call(x, mod, l, 0, n1, f1_in, f1_out)
        a_out, q, k, v, qi, ki, wiT = _mix_call(x, mod, l, nm, w_mix_in, sgu_w, sgu_bT, rope_tab)
        b_out = _attn_call(q, k, v, qi, ki, wiT)
        x = _ffn_call(x, mod, l, 6, n2, f2_in, f2_out, mix=(a_out, b_out, w_mix_out),
                      final_w=fw if l == n_layer - 1 else None)
    return x
```

```python
import functools

import jax
import jax.numpy as jnp
from jax import lax
from jax.experimental import pallas as pl
from jax.experimental.pallas import tpu as pltpu

F32 = jnp.float32
BF16 = jnp.bfloat16
I32 = jnp.int32

SGU_CHUNK = 128
SGU_GROUPS = 4
SGU_GROUP_DIM = 128
SGU_WIDTH = SGU_GROUPS * SGU_GROUP_DIM
ATT_HEAD_DIM = 64
ATT_HEADS = 8
ATT_WIDTH = ATT_HEADS * ATT_HEAD_DIM
IDX_HEADS = 4
IDX_HEAD_DIM = 64
TOPK_MAX = 256
ROPE_THETA = 500000.0
ROPE_DIM = ATT_HEAD_DIM // 4
ROPE_HALF = ROPE_DIM // 2
RMS_EPS = 1e-6
N_MOD = 9

OFF_SGU_U = 0
OFF_SGU_V = OFF_SGU_U + SGU_WIDTH
OFF_Q = OFF_SGU_V + SGU_WIDTH
OFF_K = OFF_Q + ATT_WIDTH
OFF_V = OFF_K + ATT_WIDTH
OFF_IQ = OFF_V + ATT_WIDTH
OFF_IK = OFF_IQ + IDX_HEADS * IDX_HEAD_DIM
OFF_IW = OFF_IK + IDX_HEAD_DIM
PROJ_WIDTH = OFF_IW + IDX_HEADS

LANES = 128
SUBLANES = 8
FF_CHUNK = 256
TOKEN_BLOCK = 512
ATT_BLOCK = 256
IDX_PAD = 384
COUNT_CHAINS = 4
VMEM_LIMIT = 56 * 1024 * 1024

LOG2_E = 1.4426950408889634
INT_MIN = -2147483648
NEG_BIAS = -1e30


def _resident(block_shape, index_map):
    return pl.BlockSpec(block_shape, index_map, pipeline_mode=pl.Buffered(1))


def _rms_norm(x, w):
    ms = jnp.mean(x * x, axis=-1, keepdims=True)
    return x * lax.rsqrt(ms + RMS_EPS) * w


def _gelu(x):
    return 0.5 * x * (1.0 + lax.erf(x * (0.5 ** 0.5)))


def _split_bf16(x):
    hi = x.astype(BF16)
    lo = (x - hi.astype(F32)).astype(BF16)
    return hi, lo


def _ada_kernel(c_ref, w_ref, b_ref, o_ref):
    ca = jax.nn.silu(c_ref[...]).astype(BF16)
    w = w_ref[...].astype(BF16)
    o_ref[...] = jnp.dot(ca, w, preferred_element_type=F32) + b_ref[...]


def _ada_call(c, ada_w, ada_b):
    n_layer, d, n_out = ada_w.shape
    b = c.shape[0]
    tn = n_out // 4
    return pl.pallas_call(
        _ada_kernel,
        grid=(n_layer, n_out // tn),
        in_specs=[
            pl.BlockSpec((b, d), lambda l, n: (0, 0)),
            pl.BlockSpec((None, d, tn), lambda l, n: (l, 0, n)),
            pl.BlockSpec((None, 1, tn), lambda l, n: (l, 0, n)),
        ],
        out_specs=pl.BlockSpec((None, b, tn), lambda l, n: (l, 0, n)),
        out_shape=jax.ShapeDtypeStruct((n_layer, b, n_out), F32),
        compiler_params=pltpu.CompilerParams(
            dimension_semantics=("arbitrary", "arbitrary"),
            vmem_limit_bytes=VMEM_LIMIT),
        name="ada_mod",
    )(c, ada_w, ada_b.reshape(n_layer, 1, n_out))


def _ffn_kernel(*refs, mod_row, has_mix, has_final):
    it = iter(refs)
    x_ref, mod_ref, nw_ref, win_ref, wout_ref = (next(it) for _ in range(5))
    if has_mix:
        a_ref, b_ref, wmix_ref = (next(it) for _ in range(3))
    if has_final:
        fw_ref = next(it)
    o_ref = next(it)
    acc_ref = next(it)
    d_ff = wout_ref.shape[0]

    x = x_ref[...]
    if has_mix:
        half = a_ref.shape[-1]
        y = jnp.dot(a_ref[...], wmix_ref[0:half, :].astype(BF16), preferred_element_type=F32)
        y = y + jnp.dot(b_ref[...], wmix_ref[half:, :].astype(BF16), preferred_element_type=F32)
        x = x + mod_ref[5:6, :] * y

    shift = mod_ref[mod_row:mod_row + 1, :]
    scale = mod_ref[mod_row + 1:mod_row + 2, :]
    gate = mod_ref[mod_row + 2:mod_row + 3, :]
    h = _rms_norm(x, nw_ref[...]) * (1.0 + scale) + shift
    hb = h.astype(BF16)

    for ci in range(d_ff // FF_CHUNK):
        cs = slice(ci * FF_CHUNK, (ci + 1) * FF_CHUNK)
        us = slice(d_ff + ci * FF_CHUNK, d_ff + (ci + 1) * FF_CHUNK)
        g = jnp.dot(hb, win_ref[:, cs].astype(BF16), preferred_element_type=F32)
        u = jnp.dot(hb, win_ref[:, us].astype(BF16), preferred_element_type=F32)
        a = (g * jax.nn.sigmoid(g) * u).astype(BF16)
        part = jnp.dot(a, wout_ref[cs, :].astype(BF16), preferred_element_type=F32)
        if ci == 0:
            acc_ref[...] = part
        else:
            acc_ref[...] += part

    out = x + 0.5 * gate * acc_ref[...]
    if has_final:
        out = _rms_norm(out, fw_ref[...])
    o_ref[...] = out


def _ffn_call(x, mod, layer, mod_row, norm_w, w_in, w_out, mix=None, final_w=None):
    b, s, d = x.shape
    tm = min(TOKEN_BLOCK, s)
    d_ff = w_out.shape[1]
    has_mix = mix is not None
    has_final = final_w is not None

    tok = lambda bi, i: (bi, i, 0)
    in_specs = [
        pl.BlockSpec((None, tm, d), tok),
        pl.BlockSpec((None, None, N_MOD, d), lambda bi, i: (layer, bi, 0, 0)),
        pl.BlockSpec((None, 1, d), lambda bi, i: (layer, 0, 0)),
        _resident((None, d, 2 * d_ff), lambda bi, i: (layer, 0, 0)),
        _resident((None, d_ff, d), lambda bi, i: (layer, 0, 0)),
    ]
    args = [x, mod, norm_w, w_in, w_out]
    if has_mix:
        a_out, b_out, w_mix = mix
        in_specs += [
            pl.BlockSpec((None, tm, a_out.shape[-1]), tok),
            pl.BlockSpec((None, tm, b_out.shape[-1]), tok),
            _resident((None, w_mix.shape[1], d), lambda bi, i: (layer, 0, 0)),
        ]
        args += [a_out, b_out, w_mix]
    if has_final:
        in_specs.append(pl.BlockSpec((1, d), lambda bi, i: (0, 0)))
        args.append(final_w)

    return pl.pallas_call(
        functools.partial(_ffn_kernel, mod_row=mod_row, has_mix=has_mix, has_final=has_final),
        grid=(b, s // tm),
        in_specs=in_specs,
        out_specs=pl.BlockSpec((None, tm, d), tok),
        out_shape=jax.ShapeDtypeStruct((b, s, d), F32),
        scratch_shapes=[pltpu.VMEM((tm, d), F32)],
        compiler_params=pltpu.CompilerParams(
            dimension_semantics=("parallel", "parallel"),
            vmem_limit_bytes=VMEM_LIMIT),
        name="ffn_mix" if has_mix else "ffn",
    )(*args)


def _rope(p, tab_c, tab_s1, tab_s2):
    width = p.shape[-1]
    reps = width // LANES
    if reps > 1:
        tab_c = jnp.concatenate([tab_c] * reps, axis=1)
        tab_s1 = jnp.concatenate([tab_s1] * reps, axis=1)
        tab_s2 = jnp.concatenate([tab_s2] * reps, axis=1)
    up = pltpu.roll(p, width - ROPE_HALF, 1)
    down = pltpu.roll(p, ROPE_HALF, 1)
    return p * tab_c + up * tab_s1 + down * tab_s2


def _mix_kernel(x_ref, mod_ref, nw_ref, wm_ref, sw_ref, sb_ref, rope_ref,
                a_ref, q_ref, k_ref, v_ref, qi_ref, ki_ref, wiT_ref):
    tm = x_ref.shape[0]
    x = x_ref[...]
    h = _rms_norm(x, nw_ref[...]) * (1.0 + mod_ref[4:5, :]) + mod_ref[3:4, :]
    hb = h.astype(BF16)

    tab_c = rope_ref[0]
    tab_s1 = rope_ref[1]
    tab_s2 = rope_ref[2]

    def proj(lo, hi):
        return jnp.dot(hb, wm_ref[:, lo:hi], preferred_element_type=F32)

    u = _gelu(proj(OFF_SGU_U, OFF_SGU_V))
    va = _gelu(proj(OFF_SGU_V, OFF_Q)).astype(BF16)
    row = lax.broadcasted_iota(I32, (SGU_CHUNK, SGU_CHUNK), 0)
    col = lax.broadcasted_iota(I32, (SGU_CHUNK, SGU_CHUNK), 1)
    causal = col <= row
    for g in range(SGU_GROUPS):
        wg = jnp.where(causal, sw_ref[g], 0.0).astype(BF16)
        bias = sb_ref[:, g:g + 1]
        cs = slice(g * SGU_GROUP_DIM, (g + 1) * SGU_GROUP_DIM)
        for c in range(tm // SGU_CHUNK):
            rs = slice(c * SGU_CHUNK, (c + 1) * SGU_CHUNK)
            mixed = jnp.dot(wg, va[rs, cs], preferred_element_type=F32) + bias
            a_ref[rs, cs] = (u[rs, cs] * mixed).astype(a_ref.dtype)

    q = _rope(proj(OFF_Q, OFF_K), tab_c, tab_s1, tab_s2) * (ATT_HEAD_DIM ** -0.5 * LOG2_E)
    k = _rope(proj(OFF_K, OFF_V), tab_c, tab_s1, tab_s2)
    v = proj(OFF_V, OFF_IQ)
    ones = jnp.ones((tm, ATT_HEAD_DIM), F32)
    for hd in range(ATT_HEADS):
        hs = slice(hd * ATT_HEAD_DIM, (hd + 1) * ATT_HEAD_DIM)
        q_ref[hd] = q[:, hs].astype(q_ref.dtype)
        k_ref[hd] = k[:, hs].astype(k_ref.dtype)
        v_ref[hd] = jnp.concatenate([v[:, hs], ones], axis=1).astype(v_ref.dtype)

    pi = proj(OFF_IQ, OFF_IQ + IDX_PAD)
    n_iq = IDX_HEADS * IDX_HEAD_DIM
    qi = _rope(pi[:, :n_iq], tab_c, tab_s1, tab_s2)
    lane = lax.broadcasted_iota(I32, (tm, LANES), 1)
    is_ki = lane < IDX_HEAD_DIM
    tail = _rope(pi[:, n_iq:], jnp.where(is_ki, tab_c, 1.0),
                 jnp.where(is_ki, tab_s1, 0.0), jnp.where(is_ki, tab_s2, 0.0))
    for hd in range(IDX_HEADS):
        hi, lo = _split_bf16(qi[:, hd * IDX_HEAD_DIM:(hd + 1) * IDX_HEAD_DIM])
        qi_ref[hd] = jnp.concatenate([hi, hi, lo, lo], axis=1)
    hi, lo = _split_bf16(tail[:, :IDX_HEAD_DIM])
    ki_ref[...] = jnp.concatenate([hi, lo, hi, lo], axis=1)
    wiT_ref[...] = tail.T[IDX_HEAD_DIM:IDX_HEAD_DIM + IDX_HEADS, :]


def _mix_call(x, mod, layer, norm_w, w_in, sgu_w, sgu_bT, rope_tab):
    b, s, d = x.shape
    tm = min(TOKEN_BLOCK, s)
    tok = lambda bi, i: (bi, i, 0)
    head_major = lambda bi, i: (bi, 0, i, 0)
    out_shape = [
        jax.ShapeDtypeStruct((b, s, SGU_WIDTH), BF16),
        jax.ShapeDtypeStruct((b, ATT_HEADS, s, ATT_HEAD_DIM), BF16),
        jax.ShapeDtypeStruct((b, ATT_HEADS, s, ATT_HEAD_DIM), BF16),
        jax.ShapeDtypeStruct((b, ATT_HEADS, s, 2 * ATT_HEAD_DIM), BF16),
        jax.ShapeDtypeStruct((b, IDX_HEADS, s, 4 * IDX_HEAD_DIM), BF16),
        jax.ShapeDtypeStruct((b, s, 4 * IDX_HEAD_DIM), BF16),
        jax.ShapeDtypeStruct((b, IDX_HEADS, s), F32),
    ]
    out_specs = [
        pl.BlockSpec((None, tm, SGU_WIDTH), tok),
        pl.BlockSpec((None, ATT_HEADS, tm, ATT_HEAD_DIM), head_major),
        pl.BlockSpec((None, ATT_HEADS, tm, ATT_HEAD_DIM), head_major),
        pl.BlockSpec((None, ATT_HEADS, tm, 2 * ATT_HEAD_DIM), head_major),
        pl.BlockSpec((None, IDX_HEADS, tm, 4 * IDX_HEAD_DIM), head_major),
        pl.BlockSpec((None, tm, 4 * IDX_HEAD_DIM), tok),
        pl.BlockSpec((None, IDX_HEADS, tm), lambda bi, i: (bi, 0, i)),
    ]
    in_specs = [
        pl.BlockSpec((None, tm, d), tok),
        pl.BlockSpec((None, None, N_MOD, d), lambda bi, i: (layer, bi, 0, 0)),
        pl.BlockSpec((None, 1, d), lambda bi, i: (layer, 0, 0)),
        _resident((None, d, OFF_IQ + IDX_PAD), lambda bi, i: (layer, 0, 0)),
        _resident((None, SGU_GROUPS, SGU_CHUNK, SGU_CHUNK), lambda bi, i: (layer, 0, 0, 0)),
        _resident((None, SGU_CHUNK, SGU_GROUPS), lambda bi, i: (layer, 0, 0)),
        pl.BlockSpec((None, 3, tm, LANES), lambda bi, i: (bi, 0, i, 0)),
    ]
    return pl.pallas_call(
        _mix_kernel,
        grid=(b, s // tm),
        in_specs=in_specs,
        out_specs=out_specs,
        out_shape=out_shape,
        compiler_params=pltpu.CompilerParams(
            dimension_semantics=("parallel", "parallel"),
            vmem_limit_bytes=VMEM_LIMIT),
        name="mix_proj",
    )(x, mod, norm_w, w_in, sgu_w, sgu_bT, rope_tab)


def _ordered_bits_to_float(u):
    key = u ^ INT_MIN
    bits = key ^ ((key >> 31) & 0x7FFFFFFF)
    return lax.bitcast_convert_type(bits, F32)


def _floor_bf16(s):
    hb = s.astype(BF16)
    hf = hb.astype(F32)
    bits = lax.bitcast_convert_type(hf, I32)
    step = 1 << 16
    down_bits = bits + jnp.where(hf > 0, -step, step)
    down = lax.bitcast_convert_type(down_bits, F32).astype(BF16)
    return jnp.where(hf > s, down, hb)


def _attn_kernel(q_ref, k_ref, v_ref, qi_ref, ki_ref, wiT_ref, o_ref,
                 sc_ref, sb_ref, bias_ref, m_ref, acc_ref, *, topk, seq_bits):
    tq = q_ref.shape[1]
    kc = tq
    n_head, hd = q_ref.shape[0], q_ref.shape[2]
    j = pl.program_id(1)
    n_tiles = j + 1
    q_start = j * tq
    nt_dims = (((1,), (1,)), ((), ()))

    qi_all = qi_ref[...].reshape(IDX_HEADS * tq, qi_ref.shape[-1])
    wi = wiT_ref[...]
    q_idx = q_start + lax.broadcasted_iota(I32, (kc, tq), 1)
    k_iota = lax.broadcasted_iota(I32, (kc, tq), 0)
    full = q_start + lax.broadcasted_iota(I32, (1, tq), 1) + 1 >= topk

    def select_keys(n):
        for kt in range(n):
            logits = lax.dot_general(ki_ref[kt * kc:(kt + 1) * kc, :], qi_all, nt_dims,
                                     preferred_element_type=F32)
            sc = wi[0:1, :] * jnp.maximum(logits[:, 0:tq], 0.0)
            for h in range(1, IDX_HEADS):
                sc = sc + wi[h:h + 1, :] * jnp.maximum(logits[:, h * tq:(h + 1) * tq], 0.0)
            if kt == n - 1:
                sc = jnp.where(kt * kc + k_iota <= q_idx, sc, -jnp.inf)
            sc_ref[kt] = sc
            sb_ref[kt] = _floor_bf16(sc)

        def count(pred):
            acc = jnp.zeros((COUNT_CHAINS, SUBLANES, tq), I32)
            for kt in range(n):
                ind = jnp.where(pred(sc_ref[kt], kt * kc + k_iota), 1, 0)
                acc = acc + jnp.sum(ind.reshape(COUNT_CHAINS, -1, SUBLANES, tq), axis=1)
            return jnp.sum(acc.reshape(-1, tq), axis=0, keepdims=True)

        def bisect(i, t):
            cand = t | jnp.left_shift(jnp.int32(1), 31 - i)
            cand_f = _ordered_bits_to_float(cand)
            cnt = count(lambda s, _: s >= cand_f)
            return jnp.where(cnt >= topk, cand, t)

        def count_bf16(cand_b):
            acc = jnp.zeros((COUNT_CHAINS, 2 * SUBLANES, tq), BF16)
            one, zero = jnp.ones((), BF16), jnp.zeros((), BF16)
            for kt in range(n):
                ind = jnp.where(sb_ref[kt] >= cand_b, one, zero)
                ind = ind.reshape(COUNT_CHAINS, -1, 2 * SUBLANES, tq)
                for g in range(ind.shape[1]):
                    acc = acc + ind[:, g]
            return jnp.sum(acc.astype(F32).reshape(-1, tq), axis=0, keepdims=True)

        def bisect_bf16(i, t):
            cand = t | jnp.left_shift(jnp.int32(1), 31 - i)
            cnt = count_bf16(_ordered_bits_to_float(cand).astype(BF16))
            return jnp.where(cnt >= topk, cand, t)

        t_hi = lax.fori_loop(0, 16, bisect_bf16, jnp.zeros((1, tq), I32))
        t_bits = lax.fori_loop(16, 32, bisect, t_hi)
        thr = jnp.where(full, _ordered_bits_to_float(t_bits), jnp.finfo(F32).min)
        n_ge = count(lambda s, _: s >= thr)
        tie_rows = jnp.where(full & (n_ge > topk), 1, 0)

        def mask_plain():
            for kt in range(n):
                bias_ref[kt] = jnp.where(sc_ref[kt] >= thr, 0.0, NEG_BIAS).T
            return 0

        def mask_with_ties():
            need = topk - count(lambda s, _: s > thr)

            def step(i, p):
                cand = p | jnp.left_shift(jnp.int32(1), seq_bits - 1 - i)
                cnt = count(lambda s, ki: (s == thr) & (ki < cand))
                return jnp.where(cnt < need, cand, p)

            last_tie = lax.fori_loop(0, seq_bits, step, jnp.zeros((1, tq), I32))
            last_tie = jnp.where(full, last_tie, (1 << seq_bits) - 1)
            for kt in range(n):
                s = sc_ref[kt]
                sel = (s > thr) | ((s == thr) & (kt * kc + k_iota <= last_tie))
                bias_ref[kt] = jnp.where(sel, 0.0, NEG_BIAS).T
            return 0

        return lax.cond(jnp.max(tie_rows) > 0, mask_with_ties, mask_plain)

    lax.switch(j, [functools.partial(select_keys, n + 1) for n in range(bias_ref.shape[0])])

    m_ref[...] = jnp.full_like(m_ref, -jnp.inf)
    acc_ref[...] = jnp.zeros_like(acc_ref)

    def attend_tile(kt):
        k0 = pl.multiple_of(kt * kc, kc)
        bias = bias_ref[kt]
        for h in range(n_head):
            s = lax.dot_general(q_ref[h], k_ref[h, pl.ds(k0, kc), :], nt_dims,
                                preferred_element_type=F32) + bias
            m_prev = m_ref[h]
            m_new = jnp.maximum(m_prev, jnp.max(s, axis=1, keepdims=True))
            alpha = jnp.exp2(m_prev - m_new)
            p = jnp.exp2(s - jnp.concatenate([m_new] * (kc // LANES), axis=1))
            pv = jnp.dot(p.astype(BF16), v_ref[h, pl.ds(k0, kc), :],
                         preferred_element_type=F32)
            acc_ref[h] = acc_ref[h] * alpha + pv
            m_ref[h] = m_new

    single = n_tiles % 2
    pair = (n_tiles // 2) % 2

    @pl.when(single == 1)
    def _():
        attend_tile(0)

    @pl.when(pair == 1)
    def _():
        attend_tile(single)
        attend_tile(single + 1)

    def attend_quad(kq, carry):
        for t in range(4):
            attend_tile(single + 2 * pair + 4 * kq + t)
        return carry

    lax.fori_loop(0, n_tiles // 4, attend_quad, 0)

    outs = []
    for h in range(n_head):
        acc = acc_ref[h]
        outs.append(acc[:, :hd] / acc[:, hd:])
    o_ref[...] = jnp.concatenate(outs, axis=1).astype(o_ref.dtype)


def _attn_call(q, k, v, qi, ki, wiT):
    b, n_head, s, hd = q.shape
    tq = min(ATT_BLOCK, s)
    n_blk = s // tq
    topk = min(TOPK_MAX, s // 4)
    seq_bits = max(1, (s - 1).bit_length())
    blk = lambda bi, j: (bi, 0, j, 0)
    whole = lambda bi, j: (bi, 0, 0, 0)
    return pl.pallas_call(
        functools.partial(_attn_kernel, topk=topk, seq_bits=seq_bits),
        grid=(b, n_blk),
        in_specs=[
            pl.BlockSpec((None, n_head, tq, hd), blk),
            pl.BlockSpec((None, n_head, s, hd), whole),
            pl.BlockSpec((None, n_head, s, v.shape[-1]), whole),
            pl.BlockSpec((None, IDX_HEADS, tq, qi.shape[-1]), blk),
            pl.BlockSpec((None, s, ki.shape[-1]), lambda bi, j: (bi, 0, 0)),
            pl.BlockSpec((None, IDX_HEADS, tq), lambda bi, j: (bi, 0, j)),
        ],
        out_specs=pl.BlockSpec((None, tq, n_head * hd), lambda bi, j: (bi, j, 0)),
        out_shape=jax.ShapeDtypeStruct((b, s, n_head * hd), BF16),
        scratch_shapes=[
            pltpu.VMEM((n_blk, tq, tq), F32),
            pltpu.VMEM((n_blk, tq, tq), BF16),
            pltpu.VMEM((n_blk, tq, tq), F32),
            pltpu.VMEM((n_head, tq, LANES), F32),
            pltpu.VMEM((n_head, tq, v.shape[-1]), F32),
        ],
        compiler_params=pltpu.CompilerParams(
            dimension_semantics=("parallel", "arbitrary"),
            vmem_limit_bytes=VMEM_LIMIT),
        name="dsa_attention",
    )(q, k, v, qi, ki, wiT)


def _rope_tables(positions):
    inv_freq = ROPE_THETA ** (-jnp.arange(0, ROPE_DIM, 2, dtype=F32) / ROPE_DIM)
    ang = positions.astype(F32)[..., None] * inv_freq
    cos, sin = jnp.cos(ang), jnp.sin(ang)
    rest = ATT_HEAD_DIM - ROPE_DIM
    ones = jnp.ones(cos.shape[:-1] + (rest,), F32)
    zeros = jnp.zeros(cos.shape[:-1] + (rest,), F32)
    zhalf = jnp.zeros_like(sin)
    tab_c = jnp.concatenate([cos, cos, ones], axis=-1)
    tab_s1 = jnp.concatenate([-sin, zhalf, zeros], axis=-1)
    tab_s2 = jnp.concatenate([zhalf, sin, zeros], axis=-1)
    tabs = jnp.stack([tab_c, tab_s1, tab_s2], axis=1)
    return jnp.concatenate([tabs] * (LANES // ATT_HEAD_DIM), axis=-1)


def kernel(x, c, positions, ada_w, ada_b, norm_ffn1, ffn1_w_in, ffn1_w_out, norm_mix, mix_w_in,
           sgu_w, sgu_b, mix_w_out, norm_ffn2, ffn2_w_in, ffn2_w_out, final_norm):
    n_layer, d = norm_ffn1.shape
    b = x.shape[0]

    mod = _ada_call(c, ada_w, ada_b).reshape(n_layer, b, N_MOD, d)
    rope_tab = _rope_tables(positions)

    f1_in, f1_out = ffn1_w_in, ffn1_w_out
    f2_in, f2_out = ffn2_w_in, ffn2_w_out
    w_mix_in = jnp.pad(mix_w_in.astype(BF16), ((0, 0), (0, 0), (0, OFF_IQ + IDX_PAD - PROJ_WIDTH)))
    w_mix_out = mix_w_out
    sgu_bT = sgu_b.transpose(0, 2, 1)
    n1 = norm_ffn1.reshape(n_layer, 1, d)
    nm = norm_mix.reshape(n_layer, 1, d)
    n2 = norm_ffn2.reshape(n_layer, 1, d)
    fw = final_norm.reshape(1, d)

    for l in range(n_layer):
        x = _ffn_call(x, mod, l, 0, n1, f1_in, f1_out)
        a_out, q, k, v, qi, ki, wiT = _mix_call(x, mod, l, nm, w_mix_in, sgu_w, sgu_bT, rope_tab)
        b_out = _attn_call(q, k, v, qi, ki, wiT)
        x = _ffn_call(x, mod, l, 6, n2, f2_in, f2_out, mix=(a_out, b_out, w_mix_out),
                      final_w=fw if l == n_layer - 1 else None)
    return x
```

```python
import functools

import jax
import jax.numpy as jnp
from jax import lax
from jax.experimental import pallas as pl
from jax.experimental.pallas import tpu as pltpu

F32 = jnp.float32
BF16 = jnp.bfloat16
I32 = jnp.int32

SGU_CHUNK = 128
SGU_GROUPS = 4
SGU_GROUP_DIM = 128
SGU_WIDTH = SGU_GROUPS * SGU_GROUP_DIM
ATT_HEAD_DIM = 64
ATT_HEADS = 8
ATT_WIDTH = ATT_HEADS * ATT_HEAD_DIM
IDX_HEADS = 4
IDX_HEAD_DIM = 64
TOPK_MAX = 256
ROPE_THETA = 500000.0
ROPE_DIM = ATT_HEAD_DIM // 4
ROPE_HALF = ROPE_DIM // 2
RMS_EPS = 1e-6
N_MOD = 9

OFF_SGU_U = 0
OFF_SGU_V = OFF_SGU_U + SGU_WIDTH
OFF_Q = OFF_SGU_V + SGU_WIDTH
OFF_K = OFF_Q + ATT_WIDTH
OFF_V = OFF_K + ATT_WIDTH
OFF_IQ = OFF_V + ATT_WIDTH
OFF_IK = OFF_IQ + IDX_HEADS * IDX_HEAD_DIM
OFF_IW = OFF_IK + IDX_HEAD_DIM
PROJ_WIDTH = OFF_IW + IDX_HEADS

LANES = 128
SUBLANES = 8
FF_CHUNK = 256
TOKEN_BLOCK = 512
ATT_BLOCK = 256
IDX_PAD = 384
COUNT_CHAINS = 4
VMEM_LIMIT = 56 * 1024 * 1024

LOG2_E = 1.4426950408889634
INT_MIN = -2147483648
NEG_BIAS = -1e30


def _resident(block_shape, index_map):
    return pl.BlockSpec(block_shape, index_map, pipeline_mode=pl.Buffered(1))


def _rms_norm(x, w):
    ms = jnp.mean(x * x, axis=-1, keepdims=True)
    return x * lax.rsqrt(ms + RMS_EPS) * w


def _gelu(x):
    return 0.5 * x * (1.0 + lax.erf(x * (0.5 ** 0.5)))


def _split_bf16(x):
    hi = x.astype(BF16)
    lo = (x - hi.astype(F32)).astype(BF16)
    return hi, lo


def _ada_kernel(c_ref, w_ref, b_ref, o_ref):
    ca = jax.nn.silu(c_ref[...]).astype(BF16)
    w = w_ref[...].astype(BF16)
    o_ref[...] = jnp.dot(ca, w, preferred_element_type=F32) + b_ref[...]


def _ada_call(c, ada_w, ada_b):
    n_layer, d, n_out = ada_w.shape
    b = c.shape[0]
    tn = n_out // 4
    return pl.pallas_call(
        _ada_kernel,
        grid=(n_layer, n_out // tn),
        in_specs=[
            pl.BlockSpec((b, d), lambda l, n: (0, 0)),
            pl.BlockSpec((None, d, tn), lambda l, n: (l, 0, n)),
            pl.BlockSpec((None, 1, tn), lambda l, n: (l, 0, n)),
        ],
        out_specs=pl.BlockSpec((None, b, tn), lambda l, n: (l, 0, n)),
        out_shape=jax.ShapeDtypeStruct((n_layer, b, n_out), F32),
        compiler_params=pltpu.CompilerParams(
            dimension_semantics=("arbitrary", "arbitrary"),
            vmem_limit_bytes=VMEM_LIMIT),
        name="ada_mod",
    )(c, ada_w, ada_b.reshape(n_layer, 1, n_out))


def _ffn_kernel(*refs, mod_row, has_mix, has_final):
    it = iter(refs)
    x_ref, mod_ref, nw_ref, win_ref, wout_ref = (next(it) for _ in range(5))
    if has_mix:
        a_ref, b_ref, wmix_ref = (next(it) for _ in range(3))
    if has_final:
        fw_ref = next(it)
    o_ref = next(it)
    acc_ref = next(it)
    d_ff = wout_ref.shape[0]

    x = x_ref[...]
    if has_mix:
        half = a_ref.shape[-1]
        y = jnp.dot(a_ref[...], wmix_ref[0:half, :].astype(BF16), preferred_element_type=F32)
        y = y + jnp.dot(b_ref[...], wmix_ref[half:, :].astype(BF16), preferred_element_type=F32)
        x = x + mod_ref[5:6, :] * y

    shift = mod_ref[mod_row:mod_row + 1, :]
    scale = mod_ref[mod_row + 1:mod_row + 2, :]
    gate = mod_ref[mod_row + 2:mod_row + 3, :]
    h = _rms_norm(x, nw_ref[...]) * (1.0 + scale) + shift
    hb = h.astype(BF16)

    for ci in range(d_ff // FF_CHUNK):
        cs = slice(ci * FF_CHUNK, (ci + 1) * FF_CHUNK)
        us = slice(d_ff + ci * FF_CHUNK, d_ff + (ci + 1) * FF_CHUNK)
        g = jnp.dot(hb, win_ref[:, cs].astype(BF16), preferred_element_type=F32)
        u = jnp.dot(hb, win_ref[:, us].astype(BF16), preferred_element_type=F32)
        a = (g * jax.nn.sigmoid(g) * u).astype(BF16)
        part = jnp.dot(a, wout_ref[cs, :].astype(BF16), preferred_element_type=F32)
        if ci == 0:
            acc_ref[...] = part
        else:
            acc_ref[...] += part

    out = x + 0.5 * gate * acc_ref[...]
    if has_final:
        out = _rms_norm(out, fw_ref[...])
    o_ref[...] = out


def _ffn_call(x, mod, layer, mod_row, norm_w, w_in, w_out, mix=None, final_w=None):
    b, s, d = x.shape
    tm = min(TOKEN_BLOCK, s)
    d_ff = w_out.shape[1]
    has_mix = mix is not None
    has_final = final_w is not None

    tok = lambda bi, i: (bi, i, 0)
    in_specs = [
        pl.BlockSpec((None, tm, d), tok),
        pl.BlockSpec((None, None, N_MOD, d), lambda bi, i: (layer, bi, 0, 0)),
        pl.BlockSpec((None, 1, d), lambda bi, i: (layer, 0, 0)),
        _resident((None, d, 2 * d_ff), lambda bi, i: (layer, 0, 0)),
        _resident((None, d_ff, d), lambda bi, i: (layer, 0, 0)),
    ]
    args = [x, mod, norm_w, w_in, w_out]
    if has_mix:
        a_out, b_out, w_mix = mix
        in_specs += [
            pl.BlockSpec((None, tm, a_out.shape[-1]), tok),
            pl.BlockSpec((None, tm, b_out.shape[-1]), tok),
            _resident((None, w_mix.shape[1], d), lambda bi, i: (layer, 0, 0)),
        ]
        args += [a_out, b_out, w_mix]
    if has_final:
        in_specs.append(pl.BlockSpec((1, d), lambda bi, i: (0, 0)))
        args.append(final_w)

    return pl.pallas_call(
        functools.partial(_ffn_kernel, mod_row=mod_row, has_mix=has_mix, has_final=has_final),
        grid=(b, s // tm),
        in_specs=in_specs,
        out_specs=pl.BlockSpec((None, tm, d), tok),
        out_shape=jax.ShapeDtypeStruct((b, s, d), F32),
        scratch_shapes=[pltpu.VMEM((tm, d), F32)],
        compiler_params=pltpu.CompilerParams(
            dimension_semantics=("parallel", "parallel"),
            vmem_limit_bytes=VMEM_LIMIT),
        name="ffn_mix" if has_mix else "ffn",
    )(*args)


def _rope(p, tab_c, tab_s1, tab_s2):
    width = p.shape[-1]
    reps = width // LANES
    if reps > 1:
        tab_c = jnp.concatenate([tab_c] * reps, axis=1)
        tab_s1 = jnp.concatenate([tab_s1] * reps, axis=1)
        tab_s2 = jnp.concatenate([tab_s2] * reps, axis=1)
    up = pltpu.roll(p, width - ROPE_HALF, 1)
    down = pltpu.roll(p, ROPE_HALF, 1)
    return p * tab_c + up * tab_s1 + down * tab_s2


def _mix_kernel(x_ref, mod_ref, nw_ref, wm_ref, sw_ref, sb_ref, rope_ref,
                a_ref, q_ref, k_ref, v_ref, qi_ref, ki_ref, wiT_ref):
    tm = x_ref.shape[0]
    x = x_ref[...]
    h = _rms_norm(x, nw_ref[...]) * (1.0 + mod_ref[4:5, :]) + mod_ref[3:4, :]
    hb = h.astype(BF16)

    tab_c = rope_ref[0]
    tab_s1 = rope_ref[1]
    tab_s2 = rope_ref[2]

    def proj(lo, hi):
        return jnp.dot(hb, wm_ref[:, lo:hi], preferred_element_type=F32)

    u = _gelu(proj(OFF_SGU_U, OFF_SGU_V))
    va = _gelu(proj(OFF_SGU_V, OFF_Q)).astype(BF16)
    row = lax.broadcasted_iota(I32, (SGU_CHUNK, SGU_CHUNK), 0)
    col = lax.broadcasted_iota(I32, (SGU_CHUNK, SGU_CHUNK), 1)
    causal = col <= row
    for g in range(SGU_GROUPS):
        wg = jnp.where(causal, sw_ref[g], 0.0).astype(BF16)
        bias = sb_ref[:, g:g + 1]
        cs = slice(g * SGU_GROUP_DIM, (g + 1) * SGU_GROUP_DIM)
        for c in range(tm // SGU_CHUNK):
            rs = slice(c * SGU_CHUNK, (c + 1) * SGU_CHUNK)
            mixed = jnp.dot(wg, va[rs, cs], preferred_element_type=F32) + bias
            a_ref[rs, cs] = (u[rs, cs] * mixed).astype(a_ref.dtype)

    q = _rope(proj(OFF_Q, OFF_K), tab_c, tab_s1, tab_s2) * (ATT_HEAD_DIM ** -0.5 * LOG2_E)
    k = _rope(proj(OFF_K, OFF_V), tab_c, tab_s1, tab_s2)
    v = proj(OFF_V, OFF_IQ)
    ones = jnp.ones((tm, ATT_HEAD_DIM), F32)
    for hd in range(ATT_HEADS):
        hs = slice(hd * ATT_HEAD_DIM, (hd + 1) * ATT_HEAD_DIM)
        q_ref[hd] = q[:, hs].astype(q_ref.dtype)
        k_ref[hd] = k[:, hs].astype(k_ref.dtype)
        v_ref[hd] = jnp.concatenate([v[:, hs], ones], axis=1).astype(v_ref.dtype)

    pi = proj(OFF_IQ, OFF_IQ + IDX_PAD)
    n_iq = IDX_HEADS * IDX_HEAD_DIM
    qi = _rope(pi[:, :n_iq], tab_c, tab_s1, tab_s2)
    lane = lax.broadcasted_iota(I32, (tm, LANES), 1)
    is_ki = lane < IDX_HEAD_DIM
    tail = _rope(pi[:, n_iq:], jnp.where(is_ki, tab_c, 1.0),
                 jnp.where(is_ki, tab_s1, 0.0), jnp.where(is_ki, tab_s2, 0.0))
    for hd in range(IDX_HEADS):
        hi, lo = _split_bf16(qi[:, hd * IDX_HEAD_DIM:(hd + 1) * IDX_HEAD_DIM])
        qi_ref[hd] = jnp.concatenate([hi, hi, lo, lo], axis=1)
    hi, lo = _split_bf16(tail[:, :IDX_HEAD_DIM])
    ki_ref[...] = jnp.concatenate([hi, lo, hi, lo], axis=1)
    wiT_ref[...] = tail.T[IDX_HEAD_DIM:IDX_HEAD_DIM + IDX_HEADS, :]


def _mix_call(x, mod, layer, norm_w, w_in, sgu_w, sgu_bT, rope_tab):
    b, s, d = x.shape
    tm = min(TOKEN_BLOCK, s)
    tok = lambda bi, i: (bi, i, 0)
    head_major = lambda bi, i: (bi, 0, i, 0)
    out_shape = [
        jax.ShapeDtypeStruct((b, s, SGU_WIDTH), BF16),
        jax.ShapeDtypeStruct((b, ATT_HEADS, s, ATT_HEAD_DIM), BF16),
        jax.ShapeDtypeStruct((b, ATT_HEADS, s, ATT_HEAD_DIM), BF16),
        jax.ShapeDtypeStruct((b, ATT_HEADS, s, 2 * ATT_HEAD_DIM), BF16),
        jax.ShapeDtypeStruct((b, IDX_HEADS, s, 4 * IDX_HEAD_DIM), BF16),
        jax.ShapeDtypeStruct((b, s, 4 * IDX_HEAD_DIM), BF16),
        jax.ShapeDtypeStruct((b, IDX_HEADS, s), F32),
    ]
    out_specs = [
        pl.BlockSpec((None, tm, SGU_WIDTH), tok),
        pl.BlockSpec((None, ATT_HEADS, tm, ATT_HEAD_DIM), head_major),
        pl.BlockSpec((None, ATT_HEADS, tm, ATT_HEAD_DIM), head_major),
        pl.BlockSpec((None, ATT_HEADS, tm, 2 * ATT_HEAD_DIM), head_major),
        pl.BlockSpec((None, IDX_HEADS, tm, 4 * IDX_HEAD_DIM), head_major),
        pl.BlockSpec((None, tm, 4 * IDX_HEAD_DIM), tok),
        pl.BlockSpec((None, IDX_HEADS, tm), lambda bi, i: (bi, 0, i)),
    ]
    in_specs = [
        pl.BlockSpec((None, tm, d), tok),
        pl.BlockSpec((None, None, N_MOD, d), lambda bi, i: (layer, bi, 0, 0)),
        pl.BlockSpec((None, 1, d), lambda bi, i: (layer, 0, 0)),
        _resident((None, d, OFF_IQ + IDX_PAD), lambda bi, i: (layer, 0, 0)),
        _resident((None, SGU_GROUPS, SGU_CHUNK, SGU_CHUNK), lambda bi, i: (layer, 0, 0, 0)),
        _resident((None, SGU_CHUNK, SGU_GROUPS), lambda bi, i: (layer, 0, 0)),
        pl.BlockSpec((None, 3, tm, LANES), lambda bi, i: (bi, 0, i, 0)),
    ]
    return pl.pallas_call(
        _mix_kernel,
        grid=(b, s // tm),
        in_specs=in_specs,
        out_specs=out_specs,
        out_shape=out_shape,
        compiler_params=pltpu.CompilerParams(
            dimension_semantics=("parallel", "parallel"),
            vmem_limit_bytes=VMEM_LIMIT),
        name="mix_proj",
    )(x, mod, norm_w, w_in, sgu_w, sgu_bT, rope_tab)


def _ordered_bits_to_float(u):
    key = u ^ INT_MIN
    bits = key ^ ((key >> 31) & 0x7FFFFFFF)
    return lax.bitcast_convert_type(bits, F32)


def _floor_bf16(s):
    hb = s.astype(BF16)
    hf = hb.astype(F32)
    bits = lax.bitcast_convert_type(hf, I32)
    step = 1 << 16
    down_bits = bits + jnp.where(hf > 0, -step, step)
    down = lax.bitcast_convert_type(down_bits, F32).astype(BF16)
    return jnp.where(hf > s, down, hb)


def _attn_kernel(q_ref, k_ref, v_ref, qi_ref, ki_ref, wiT_ref, o_ref,
                 sc_ref, sb_ref, bias_ref, m_ref, acc_ref, *, topk, seq_bits):
    tq = q_ref.shape[1]
    kc = tq
    n_head, hd = q_ref.shape[0], q_ref.shape[2]
    j = pl.program_id(1)
    n_tiles = j + 1
    q_start = j * tq
    nt_dims = (((1,), (1,)), ((), ()))

    qi_all = qi_ref[...].reshape(IDX_HEADS * tq, qi_ref.shape[-1])
    wi = wiT_ref[...]
    q_idx = q_start + lax.broadcasted_iota(I32, (kc, tq), 1)
    k_iota = lax.broadcasted_iota(I32, (kc, tq), 0)
    full = q_start + lax.broadcasted_iota(I32, (1, tq), 1) + 1 >= topk

    def select_keys(n):
        for kt in range(n):
            logits = lax.dot_general(ki_ref[kt * kc:(kt + 1) * kc, :], qi_all, nt_dims,
                                     preferred_element_type=F32)
            sc = wi[0:1, :] * jnp.maximum(logits[:, 0:tq], 0.0)
            for h in range(1, IDX_HEADS):
                sc = sc + wi[h:h + 1, :] * jnp.maximum(logits[:, h * tq:(h + 1) * tq], 0.0)
            if kt == n - 1:
                sc = jnp.where(kt * kc + k_iota <= q_idx, sc, -jnp.inf)
            sc_ref[kt] = sc
            sb_ref[kt] = _floor_bf16(sc)

        def count(pred):
            acc = jnp.zeros((COUNT_CHAINS, SUBLANES, tq), I32)
            for kt in range(n):
                ind = jnp.where(pred(sc_ref[kt], kt * kc + k_iota), 1, 0)
                acc = acc + jnp.sum(ind.reshape(COUNT_CHAINS, -1, SUBLANES, tq), axis=1)
            return jnp.sum(acc.reshape(-1, tq), axis=0, keepdims=True)

        def bisect(i, carry):
            t, n_ge = carry
            cand = t | jnp.left_shift(jnp.int32(1), 31 - i)
            cand_f = _ordered_bits_to_float(cand)
            cnt = count(lambda s, _: s >= cand_f)
            take = cnt >= topk
            return jnp.where(take, cand, t), jnp.where(take, cnt.astype(F32), n_ge)

        def count_bf16(cand_b):
            acc = jnp.zeros((COUNT_CHAINS, 2 * SUBLANES, tq), BF16)
            one, zero = jnp.ones((), BF16), jnp.zeros((), BF16)
            for kt in range(n):
                ind = jnp.where(sb_ref[kt] >= cand_b, one, zero)
                ind = ind.reshape(COUNT_CHAINS, -1, 2 * SUBLANES, tq)
                for g in range(ind.shape[1]):
                    acc = acc + ind[:, g]
            return jnp.sum(acc.astype(F32).reshape(-1, tq), axis=0, keepdims=True)

        def bisect_bf16(i, carry):
            t, n_ge = carry
            cand = t | jnp.left_shift(jnp.int32(1), 31 - i)
            cnt = count_bf16(_ordered_bits_to_float(cand).astype(BF16))
            take = cnt >= topk
            return jnp.where(take, cand, t), jnp.where(take, cnt, n_ge)

        start = (jnp.zeros((1, tq), I32), jnp.full((1, tq), n * kc, F32))
        t_bits, n_ge = lax.fori_loop(16, 32, bisect, lax.fori_loop(0, 16, bisect_bf16, start))
        thr = jnp.where(full, _ordered_bits_to_float(t_bits), jnp.finfo(F32).min)
        tie_rows = jnp.where(full & (n_ge > topk), 1, 0)

        def mask_plain():
            for kt in range(n):
                bias_ref[kt] = jnp.where(sc_ref[kt] >= thr, 0.0, NEG_BIAS).T
            return 0

        def mask_with_ties():
            need = topk - count(lambda s, _: s > thr)

            def step(i, p):
                cand = p | jnp.left_shift(jnp.int32(1), seq_bits - 1 - i)
                cnt = count(lambda s, ki: (s == thr) & (ki < cand))
                return jnp.where(cnt < need, cand, p)

            last_tie = lax.fori_loop(0, seq_bits, step, jnp.zeros((1, tq), I32))
            last_tie = jnp.where(full, last_tie, (1 << seq_bits) - 1)
            for kt in range(n):
                s = sc_ref[kt]
                sel = (s > thr) | ((s == thr) & (kt * kc + k_iota <= last_tie))
                bias_ref[kt] = jnp.where(sel, 0.0, NEG_BIAS).T
            return 0

        return lax.cond(jnp.max(tie_rows) > 0, mask_with_ties, mask_plain)

    lax.switch(j, [functools.partial(select_keys, n + 1) for n in range(bias_ref.shape[0])])

    m_ref[...] = jnp.full_like(m_ref, -jnp.inf)
    acc_ref[...] = jnp.zeros_like(acc_ref)

    def attend_tile(kt):
        k0 = pl.multiple_of(kt * kc, kc)
        bias = bias_ref[kt]
        for h in range(n_head):
            s = lax.dot_general(q_ref[h], k_ref[h, pl.ds(k0, kc), :], nt_dims,
                                preferred_element_type=F32) + bias
            m_prev = m_ref[h]
            m_new = jnp.maximum(m_prev, jnp.max(s, axis=1, keepdims=True))
            alpha = jnp.exp2(m_prev - m_new)
            p = jnp.exp2(s - jnp.concatenate([m_new] * (kc // LANES), axis=1))
            pv = jnp.dot(p.astype(BF16), v_ref[h, pl.ds(k0, kc), :],
                         preferred_element_type=F32)
            acc_ref[h] = acc_ref[h] * alpha + pv
            m_ref[h] = m_new

    single = n_tiles % 2
    pair = (n_tiles // 2) % 2

    @pl.when(single == 1)
    def _():
        attend_tile(0)

    @pl.when(pair == 1)
    def _():
        attend_tile(single)
        attend_tile(single + 1)

    def attend_quad(kq, carry):
        for t in range(4):
            attend_tile(single + 2 * pair + 4 * kq + t)
        return carry

    lax.fori_loop(0, n_tiles // 4, attend_quad, 0)

    outs = []
    for h in range(n_head):
        acc = acc_ref[h]
        outs.append(acc[:, :hd] / acc[:, hd:])
    o_ref[...] = jnp.concatenate(outs, axis=1).astype(o_ref.dtype)


def _attn_call(q, k, v, qi, ki, wiT):
    b, n_head, s, hd = q.shape
    tq = min(ATT_BLOCK, s)
    n_blk = s // tq
    topk = min(TOPK_MAX, s // 4)
    seq_bits = max(1, (s - 1).bit_length())
    blk = lambda bi, j: (bi, 0, j, 0)
    whole = lambda bi, j: (bi, 0, 0, 0)
    return pl.pallas_call(
        functools.partial(_attn_kernel, topk=topk, seq_bits=seq_bits),
        grid=(b, n_blk),
        in_specs=[
            pl.BlockSpec((None, n_head, tq, hd), blk),
            pl.BlockSpec((None, n_head, s, hd), whole),
            pl.BlockSpec((None, n_head, s, v.shape[-1]), whole),
            pl.BlockSpec((None, IDX_HEADS, tq, qi.shape[-1]), blk),
            pl.BlockSpec((None, s, ki.shape[-1]), lambda bi, j: (bi, 0, 0)),
            pl.BlockSpec((None, IDX_HEADS, tq), lambda bi, j: (bi, 0, j)),
        ],
        out_specs=pl.BlockSpec((None, tq, n_head * hd), lambda bi, j: (bi, j, 0)),
        out_shape=jax.ShapeDtypeStruct((b, s, n_head * hd), BF16),
        scratch_shapes=[
            pltpu.VMEM((n_blk, tq, tq), F32),
            pltpu.VMEM((n_blk, tq, tq), BF16),
            pltpu.VMEM((n_blk, tq, tq), F32),
            pltpu.VMEM((n_head, tq, LANES), F32),
            pltpu.VMEM((n_head, tq, v.shape[-1]), F32),
        ],
        compiler_params=pltpu.CompilerParams(
            dimension_semantics=("parallel", "arbitrary"),
            vmem_limit_bytes=VMEM_LIMIT),
        name="dsa_attention",
    )(q, k, v, qi, ki, wiT)


def _rope_tables(positions):
    inv_freq = ROPE_THETA ** (-jnp.arange(0, ROPE_DIM, 2, dtype=F32) / ROPE_DIM)
    ang = positions.astype(F32)[..., None] * inv_freq
    cos, sin = jnp.cos(ang), jnp.sin(ang)
    rest = ATT_HEAD_DIM - ROPE_DIM
    ones = jnp.ones(cos.shape[:-1] + (rest,), F32)
    zeros = jnp.zeros(cos.shape[:-1] + (rest,), F32)
    zhalf = jnp.zeros_like(sin)
    tab_c = jnp.concatenate([cos, cos, ones], axis=-1)
    tab_s1 = jnp.concatenate([-sin, zhalf, zeros], axis=-1)
    tab_s2 = jnp.concatenate([zhalf, sin, zeros], axis=-1)
    tabs = jnp.stack([tab_c, tab_s1, tab_s2], axis=1)
    return jnp.concatenate([tabs] * (LANES // ATT_HEAD_DIM), axis=-1)


def kernel(x, c, positions, ada_w, ada_b, norm_ffn1, ffn1_w_in, ffn1_w_out, norm_mix, mix_w_in,
           sgu_w, sgu_b, mix_w_out, norm_ffn2, ffn2_w_in, ffn2_w_out, final_norm):
    n_layer, d = norm_ffn1.shape
    b = x.shape[0]

    mod = _ada_call(c, ada_w, ada_b).reshape(n_layer, b, N_MOD, d)
    rope_tab = _rope_tables(positions)

    f1_in, f1_out = ffn1_w_in, ffn1_w_out
    f2_in, f2_out = ffn2_w_in, ffn2_w_out
    w_mix_in = jnp.pad(mix_w_in.astype(BF16), ((0, 0), (0, 0), (0, OFF_IQ + IDX_PAD - PROJ_WIDTH)))
    w_mix_out = mix_w_out
    sgu_bT = sgu_b.transpose(0, 2, 1)
    n1 = norm_ffn1.reshape(n_layer, 1, d)
    nm = norm_mix.reshape(n_layer, 1, d)
    n2 = norm_ffn2.reshape(n_layer, 1, d)
    fw = final_norm.reshape(1, d)

    for l in range(n_layer):
        x = _ffn_call(x, mod, l, 0, n1, f1_in, f1_out)
        a_out, q, k, v, qi, ki, wiT = _mix_call(x, mod, l, nm, w_mix_in, sgu_w, sgu_bT, rope_tab)
        b_out = _attn_call(q, k, v, qi, ki, wiT)
        x = _ffn_call(x, mod, l, 6, n2, f2_in, f2_out, mix=(a_out, b_out, w_mix_out),
                      final_w=fw if l == n_layer - 1 else None)
    return x
```
